```python
import math
import jax, jax.numpy as jnp
from jax import lax
import numpy as np

D_MODEL = 2048
BATCH = 2
SEQ = 8192
DEPTH = 4

N_MIXERS = 2
HEAD_DIM = 64
N_HEADS = D_MODEL // HEAD_DIM
SWA_KV_HEADS = 8
SWA_GROUP = N_HEADS // SWA_KV_HEADS
WINDOW = 128
BLOCK = 128
ROPE_THETA = 500000.0
ROT_DIM = HEAD_DIM // 4
D_FF = int(math.ceil((8 * D_MODEL / 3) / 256) * 256)
N_SWA = (DEPTH + 1) // 2
N_FOX = DEPTH // 2
RMS_EPS = 1e-6

kernel_name = "hybrid_swa_sink_fox_swiglu_sandwich"


def rmsnorm(x, g):
    x32 = x.astype(jnp.float32)
    y = x32 * lax.rsqrt(jnp.mean(x32 * x32, axis=-1, keepdims=True) + RMS_EPS) * g.astype(jnp.float32)
    return y.astype(x.dtype)


def rotary_tables(positions, dtype):
    inv_freq = ROPE_THETA ** (-jnp.arange(0, ROT_DIM, 2, dtype=jnp.float32) / ROT_DIM)
    ang = positions.astype(jnp.float32)[..., None] * inv_freq
    return jnp.cos(ang)[:, :, None, :].astype(dtype), jnp.sin(ang)[:, :, None, :].astype(dtype)


def partial_rotary(x, cos, sin):
    xr, xp = x[..., :ROT_DIM], x[..., ROT_DIM:]
    x1, x2 = xr[..., :ROT_DIM // 2], xr[..., ROT_DIM // 2:]
    rot = jnp.concatenate([x1 * cos - x2 * sin, x2 * cos + x1 * sin], axis=-1)
    return jnp.concatenate([rot, xp], axis=-1)


def swa_sink_attention(h, w_in, sinks, w_out, cos, sin):
    B, S, _ = h.shape
    nblk = S // BLOCK
    qkv = h @ w_in
    qd, kd = N_HEADS * HEAD_DIM, SWA_KV_HEADS * HEAD_DIM
    q = qkv[..., :qd].reshape(B, S, N_HEADS, HEAD_DIM)
    k = qkv[..., qd:qd + kd].reshape(B, S, SWA_KV_HEADS, HEAD_DIM)
    v = qkv[..., qd + kd:].reshape(B, S, SWA_KV_HEADS, HEAD_DIM)
    q = partial_rotary(q, cos, sin)
    k = partial_rotary(k, cos, sin)

    qb = q.reshape(B, nblk, BLOCK, SWA_KV_HEADS, SWA_GROUP, HEAD_DIM)

    def band(t):
        tb = t.reshape(B, nblk, BLOCK, SWA_KV_HEADS, HEAD_DIM)
        prev = jnp.pad(tb[:, :-1], ((0, 0), (1, 0), (0, 0), (0, 0), (0, 0)))
        return jnp.concatenate([prev, tb], axis=2)

    kband, vband = band(k), band(v)
    scale = HEAD_DIM ** -0.5
    s = jnp.einsum('bnqkgd,bnskd->bnkgqs', qb, kband).astype(jnp.float32) * scale

    q_loc = jnp.arange(BLOCK)[:, None]
    s_loc = jnp.arange(2 * BLOCK)[None, :]
    rel = BLOCK + q_loc - s_loc
    in_win = (rel >= 0) & (rel < WINDOW)
    has_prev = (jnp.arange(nblk) > 0)[:, None, None] | (s_loc >= BLOCK)[None]
    valid = (in_win[None] & has_prev)[None, :, None, None]
    s = jnp.where(valid, s, -jnp.inf)

    sink = sinks.astype(jnp.float32).reshape(SWA_KV_HEADS, SWA_GROUP)[None, None, :, :, None, None]
    m = jnp.maximum(jnp.max(s, axis=-1, keepdims=True), sink)
    p = jnp.exp(s - m)
    denom = jnp.sum(p, axis=-1, keepdims=True) + jnp.exp(sink - m)
    probs = (p / denom).astype(v.dtype)
    out = jnp.einsum('bnkgqs,bnskd->bnqkgd', probs, vband).reshape(B, S, N_HEADS * HEAD_DIM)
    return out @ w_out


def forgetting_attention(h, w_in, b_f, w_out):
    B, S, _ = h.shape
    nblk = S // BLOCK
    hd = N_HEADS * HEAD_DIM
    proj = h @ w_in
    q = proj[..., :hd].reshape(B, S, N_HEADS, HEAD_DIM)
    k = proj[..., hd:2 * hd].reshape(B, S, N_HEADS, HEAD_DIM)
    v = proj[..., 2 * hd:3 * hd].reshape(B, S, N_HEADS, HEAD_DIM)
    f_logit = proj[..., 3 * hd:]
    log_f = jax.nn.log_sigmoid(f_logit.astype(jnp.float32) + b_f.astype(jnp.float32))
    c = jnp.transpose(jnp.cumsum(log_f, axis=1), (0, 2, 1))
    key_pos = jnp.arange(S)
    scale = HEAD_DIM ** -0.5

    def block(i):
        start = i * BLOCK
        qi = lax.dynamic_slice_in_dim(q, start, BLOCK, axis=1)
        ci = lax.dynamic_slice_in_dim(c, start, BLOCK, axis=2)
        s = jnp.einsum('bqhd,bshd->bhqs', qi, k).astype(jnp.float32) * scale
        s = s + ci[..., :, None] - c[:, :, None, :]
        t_pos = start + jnp.arange(BLOCK)
        s = jnp.where(key_pos[None, :] <= t_pos[:, None], s, -jnp.inf)
        p = jax.nn.softmax(s, axis=-1).astype(v.dtype)
        return jnp.einsum('bhqs,bshd->bqhd', p, v)

    out = lax.map(block, jnp.arange(nblk))
    out = jnp.transpose(out, (1, 0, 2, 3, 4)).reshape(B, S, hd)
    return out @ w_out


def swiglu_ffn(h, w_gate_up, w_down):
    gu = h @ w_gate_up
    gate, up = gu[..., :D_FF], gu[..., D_FF:]
    return (jax.nn.silu(gate) * up) @ w_down


def setup_inputs(seed: int = 0) -> dict:
    key = jax.random.key(seed)
    ks = jax.random.split(key, 10)
    f32 = jnp.float32
    x = jax.random.normal(ks[0], (BATCH, SEQ, D_MODEL), f32)
    positions = jnp.broadcast_to(jnp.arange(SEQ, dtype=jnp.int32), (BATCH, SEQ)).astype(jnp.int32)
    norm_gains = 1.0 + 0.05 * jax.random.normal(ks[1], (DEPTH, 4, D_MODEL), f32)
    swa_in_dim = (N_HEADS + 2 * SWA_KV_HEADS) * HEAD_DIM
    swa_w_in = jax.random.normal(ks[2], (N_SWA, D_MODEL, swa_in_dim), f32) * D_MODEL ** -0.5
    swa_sinks = 0.5 * jax.random.normal(ks[3], (N_SWA, N_HEADS), f32)
    swa_w_out = jax.random.normal(ks[4], (N_SWA, N_HEADS * HEAD_DIM, D_MODEL), f32) * (N_HEADS * HEAD_DIM) ** -0.5
    fox_in_dim = 3 * N_HEADS * HEAD_DIM + N_HEADS
    fox_w_in = jax.random.normal(ks[5], (N_FOX, D_MODEL, fox_in_dim), f32) * D_MODEL ** -0.5
    fox_b_f = 2.0 + 0.5 * jax.random.normal(ks[6], (N_FOX, N_HEADS), f32)
    fox_w_out = jax.random.normal(ks[7], (N_FOX, N_HEADS * HEAD_DIM, D_MODEL), f32) * (N_HEADS * HEAD_DIM) ** -0.5
    ffn_w_gate_up = jax.random.normal(ks[8], (DEPTH, D_MODEL, 2 * D_FF), f32) * D_MODEL ** -0.5
    ffn_w_down = jax.random.normal(ks[9], (DEPTH, D_FF, D_MODEL), f32) * D_FF ** -0.5
    return {"x": x, "positions": positions, "norm_gains": norm_gains,
            "swa_w_in": swa_w_in, "swa_sinks": swa_sinks, "swa_w_out": swa_w_out,
            "fox_w_in": fox_w_in, "fox_b_f": fox_b_f, "fox_w_out": fox_w_out,
            "ffn_w_gate_up": ffn_w_gate_up, "ffn_w_down": ffn_w_down}


def reference(x, positions, norm_gains, swa_w_in, swa_sinks, swa_w_out,
              fox_w_in, fox_b_f, fox_w_out, ffn_w_gate_up, ffn_w_down):
    cos, sin = rotary_tables(positions, x.dtype)
    for i in range(DEPTH):
        g = norm_gains[i]
        h = rmsnorm(x, g[0])
        if i % N_MIXERS == 0:
            j = i // N_MIXERS
            y = swa_sink_attention(h, swa_w_in[j], swa_sinks[j], swa_w_out[j], cos, sin)
        else:
            j = i // N_MIXERS
            y = forgetting_attention(h, fox_w_in[j], fox_b_f[j], fox_w_out[j])
        x = x + rmsnorm(y, g[1])
        h = rmsnorm(x, g[2])
        y = swiglu_ffn(h, ffn_w_gate_up[i], ffn_w_down[i])
        x = x + rmsnorm(y, g[3])
    return x
```

```python
import functools

import jax
import jax.numpy as jnp
from jax import lax
from jax.experimental import pallas as pl
from jax.experimental.pallas import tpu as pltpu

D_MODEL = 2048
HEAD_DIM = 64
N_HEADS = D_MODEL // HEAD_DIM
SWA_KV_HEADS = 8
SWA_GROUP = N_HEADS // SWA_KV_HEADS
WINDOW = 128
BLOCK = 128
ROPE_THETA = 500000.0
ROT_DIM = HEAD_DIM // 4
RMS_EPS = 1e-6

LANES = 128
HEADS_PER_TILE = LANES // HEAD_DIM
VMEM_LIMIT_BYTES = 56 * 1024 * 1024
MASK_VALUE = -1e30

PROJ_TM = 1024
PROJ_TN = 512
OUT_TM = 512
FFN_TM = 512
FFN_TF = 512
FOX_TQ = 256
FOX_TK = 256

_F32 = jnp.float32
_BF16 = jnp.bfloat16
_NT = (((1,), (1,)), ((), ()))


def _params(semantics):
    return pltpu.CompilerParams(dimension_semantics=semantics,
                                vmem_limit_bytes=VMEM_LIMIT_BYTES)


def _rms_scale(v):
    return lax.rsqrt(jnp.mean(v * v, axis=-1, keepdims=True) + RMS_EPS)


def _split3(v):
    hi = v.astype(_BF16)
    r1 = v - hi.astype(_F32)
    mid = r1.astype(_BF16)
    lo = (r1 - mid.astype(_F32)).astype(_BF16)
    return hi, mid, lo


def _proj_swa_kernel(x_ref, g_ref, w_ref, ra_ref, rb_ref, rc_ref, o_ref, h_ref,
                     *, n_q_blocks, n_rope_blocks):
    j = pl.program_id(1)

    @pl.when(j == 0)
    def _():
        x = x_ref[...]
        h_ref[...] = (x * _rms_scale(x) * g_ref[...]).astype(_BF16)

    res = jnp.dot(h_ref[...], w_ref[...], preferred_element_type=_F32)

    @pl.when(j < n_rope_blocks)
    def _():
        scale = jnp.where(j < n_q_blocks, HEAD_DIM ** -0.5, 1.0).astype(_F32)
        ra, rb, rc = ra_ref[...], rb_ref[...], rc_ref[...]
        half = ROT_DIM // 2
        for c in range(res.shape[1] // LANES):
            r = res[:, c * LANES:(c + 1) * LANES]
            rot = (r * ra + pltpu.roll(r, LANES - half, 1) * rb
                   + pltpu.roll(r, half, 1) * rc)
            o_ref[:, c * LANES:(c + 1) * LANES] = (rot * scale).astype(_BF16)

    @pl.when(j >= n_rope_blocks)
    def _():
        o_ref[...] = res.astype(_BF16)


def _proj_swa(x2, gain, w_bf16, ra, rb, rc):
    t, d = x2.shape
    n = w_bf16.shape[1]
    qd = N_HEADS * HEAD_DIM
    kd = SWA_KV_HEADS * HEAD_DIM
    kern = functools.partial(_proj_swa_kernel, n_q_blocks=qd // PROJ_TN,
                             n_rope_blocks=(qd + kd) // PROJ_TN)
    return pl.pallas_call(
        kern,
        grid=(t // PROJ_TM, n // PROJ_TN),
        in_specs=[
            pl.BlockSpec((PROJ_TM, d), lambda i, j: (i, 0)),
            pl.BlockSpec((1, d), lambda i, j: (0, 0)),
            pl.BlockSpec((d, PROJ_TN), lambda i, j: (0, j)),
            pl.BlockSpec((PROJ_TM, LANES), lambda i, j: (i, 0)),
            pl.BlockSpec((PROJ_TM, LANES), lambda i, j: (i, 0)),
            pl.BlockSpec((PROJ_TM, LANES), lambda i, j: (i, 0)),
        ],
        out_specs=pl.BlockSpec((PROJ_TM, PROJ_TN), lambda i, j: (i, j)),
        out_shape=jax.ShapeDtypeStruct((t, n), _BF16),
        scratch_shapes=[pltpu.VMEM((PROJ_TM, d), _BF16)],
        compiler_params=_params(("arbitrary", "arbitrary")),
        name="proj_swa",
    )(x2, gain, w_bf16, ra, rb, rc)


def _proj_fox_kernel(x_ref, g_ref, w_ref, wf_ref, bf_ref, tri_ref, o_ref, c_ref,
                     h_ref, carry_ref, *, n_q_blocks, blocks_per_seq):
    i = pl.program_id(0)
    j = pl.program_id(1)

    @pl.when(j == 0)
    def _():
        x = x_ref[...]
        hf = x * _rms_scale(x) * g_ref[...]
        h_hi = hf.astype(_BF16)
        h_ref[...] = h_hi
        h_lo = (hf - h_hi.astype(_F32)).astype(_BF16)
        wf = wf_ref[...]
        w_hi = wf.astype(_BF16)
        w_lo = (wf - w_hi.astype(_F32)).astype(_BF16)
        logit = (jnp.dot(h_hi, w_hi, preferred_element_type=_F32)
                 + jnp.dot(h_lo, w_hi, preferred_element_type=_F32)
                 + jnp.dot(h_hi, w_lo, preferred_element_type=_F32))
        z = logit.T[:N_HEADS, :] + bf_ref[...]
        log_f = jnp.minimum(z, 0.0) - jnp.log1p(jnp.exp(-jnp.abs(z)))
        tri = tri_ref[...]
        csum = sum(jnp.dot(part, tri, preferred_element_type=_F32)
                   for part in _split3(log_f))

        @pl.when(i % blocks_per_seq == 0)
        def _():
            carry_ref[...] = jnp.zeros_like(carry_ref)

        c = csum + carry_ref[...]
        c_ref[...] = c
        carry_ref[...] = c[:, -1:]

    res = jnp.dot(h_ref[...], w_ref[...], preferred_element_type=_F32)

    @pl.when(j < n_q_blocks)
    def _():
        o_ref[...] = (res * HEAD_DIM ** -0.5).astype(_BF16)

    @pl.when(j >= n_q_blocks)
    def _():
        o_ref[...] = res.astype(_BF16)


def _proj_fox(x2, gain, w_bf16, wf_pad, b_f, seq_len):
    t, d = x2.shape
    n = w_bf16.shape[1]
    tri = (lax.broadcasted_iota(jnp.int32, (PROJ_TM, PROJ_TM), 0)
           <= lax.broadcasted_iota(jnp.int32, (PROJ_TM, PROJ_TM), 1)).astype(_BF16)
    kern = functools.partial(_proj_fox_kernel,
                             n_q_blocks=(N_HEADS * HEAD_DIM) // PROJ_TN,
                             blocks_per_seq=seq_len // PROJ_TM)
    return pl.pallas_call(
        kern,
        grid=(t // PROJ_TM, n // PROJ_TN),
        in_specs=[
            pl.BlockSpec((PROJ_TM, d), lambda i, j: (i, 0)),
            pl.BlockSpec((1, d), lambda i, j: (0, 0)),
            pl.BlockSpec((d, PROJ_TN), lambda i, j: (0, j)),
            pl.BlockSpec((d, LANES), lambda i, j: (0, 0)),
            pl.BlockSpec((N_HEADS, 1), lambda i, j: (0, 0)),
            pl.BlockSpec((PROJ_TM, PROJ_TM), lambda i, j: (0, 0)),
        ],
        out_specs=[
            pl.BlockSpec((PROJ_TM, PROJ_TN), lambda i, j: (i, j)),
            pl.BlockSpec((N_HEADS, PROJ_TM), lambda i, j: (0, i)),
        ],
        out_shape=[jax.ShapeDtypeStruct((t, n), _BF16),
                   jax.ShapeDtypeStruct((N_HEADS, t), _F32)],
        scratch_shapes=[pltpu.VMEM((PROJ_TM, d), _BF16),
                        pltpu.VMEM((N_HEADS, 1), _F32)],
        compiler_params=_params(("arbitrary", "arbitrary")),
        name="proj_fox",
    )(x2, gain, w_bf16, wf_pad, b_f, tri)


def _swa_kernel(sink_ref, q_ref, kp_ref, kc_ref, vp_ref, vc_ref, o_ref):
    g = pl.program_id(1)
    n = pl.program_id(2)
    lane = lax.broadcasted_iota(jnp.int32, (1, LANES), 1)
    low = lane < HEAD_DIM

    def both_halves(prev_ref, cur_ref):
        band = jnp.concatenate([prev_ref[...], cur_ref[...]], axis=0).astype(_F32)
        swapped = pltpu.roll(band, HEAD_DIM, 1)
        return (jnp.where(low, band, swapped).astype(_BF16),
                jnp.where(low, swapped, band).astype(_BF16))

    k_dup = both_halves(kp_ref, kc_ref)
    v_dup = both_halves(vp_ref, vc_ref)

    q_pos = lax.broadcasted_iota(jnp.int32, (BLOCK, 2 * BLOCK), 0)
    s_pos = lax.broadcasted_iota(jnp.int32, (BLOCK, 2 * BLOCK), 1)
    rel = BLOCK + q_pos - s_pos
    valid = (rel >= 0) & (rel < WINDOW) & ((s_pos >= BLOCK) | (n > 0))

    zero = jnp.zeros((), _BF16)
    for kv in range(2):
        chunks = [q_ref[:, (2 * kv + c) * LANES:(2 * kv + c + 1) * LANES]
                  for c in range(SWA_GROUP // HEADS_PER_TILE)]
        q_stack = jnp.concatenate(
            [jnp.where(low if hh == 0 else ~low, qc, zero)
             for qc in chunks for hh in range(HEADS_PER_TILE)], axis=0)
        s_all = lax.dot_general(q_stack, k_dup[kv], _NT, preferred_element_type=_F32)
        probs, inv = [], []
        for h in range(SWA_GROUP):
            s = jnp.where(valid, s_all[h * BLOCK:(h + 1) * BLOCK], MASK_VALUE)
            sink = sink_ref[g * 2 * SWA_GROUP + kv * SWA_GROUP + h]
            m = jnp.maximum(jnp.max(s, axis=-1, keepdims=True), sink)
            p = jnp.exp(s - m)
            denom = jnp.sum(p, axis=-1, keepdims=True) + jnp.exp(sink - m)
            probs.append(p.astype(_BF16))
            inv.append(1.0 / denom)
        o_all = jnp.dot(jnp.concatenate(probs, axis=0), v_dup[kv],
                        preferred_element_type=_F32)
        for c in range(SWA_GROUP // HEADS_PER_TILE):
            h0, h1 = HEADS_PER_TILE * c, HEADS_PER_TILE * c + 1
            o0 = o_all[h0 * BLOCK:(h0 + 1) * BLOCK] * inv[h0]
            o1 = o_all[h1 * BLOCK:(h1 + 1) * BLOCK] * inv[h1]
            col = (2 * kv + c) * LANES
            o_ref[:, col:col + LANES] = jnp.where(low, o0, o1).astype(_BF16)


def _swa_attention(qkv, sinks, batch, seq_len):
    t = qkv.shape[0]
    nblk = seq_len // BLOCK
    qd = N_HEADS * HEAD_DIM
    kd = SWA_KV_HEADS * HEAD_DIM
    q_tile = 2 * SWA_GROUP * HEAD_DIM
    k_col0 = qd // LANES
    v_col0 = (qd + kd) // LANES

    def cur(col0):
        return lambda b, g, n: (b * nblk + n, col0 + g)

    def prev(col0):
        return lambda b, g, n: (b * nblk + jnp.maximum(n - 1, 0), col0 + g)

    return pl.pallas_call(
        _swa_kernel,
        grid=(batch, SWA_KV_HEADS // 2, nblk),
        in_specs=[
            pl.BlockSpec(memory_space=pltpu.SMEM),
            pl.BlockSpec((BLOCK, q_tile), lambda b, g, n: (b * nblk + n, g)),
            pl.BlockSpec((BLOCK, LANES), prev(k_col0)),
            pl.BlockSpec((BLOCK, LANES), cur(k_col0)),
            pl.BlockSpec((BLOCK, LANES), prev(v_col0)),
            pl.BlockSpec((BLOCK, LANES), cur(v_col0)),
        ],
        out_specs=pl.BlockSpec((BLOCK, q_tile), lambda b, g, n: (b * nblk + n, g)),
        out_shape=jax.ShapeDtypeStruct((t, qd), _BF16),
        compiler_params=_params(("arbitrary", "arbitrary", "arbitrary")),
        name="swa_attn",
    )(sinks, qkv, qkv, qkv, qkv, qkv)


def _fox_kernel(q_ref, k_ref, v_ref, c_ref, o_ref):
    i = pl.program_id(2)
    tq, tk = FOX_TQ, FOX_TK
    steps_per_tile = tq // tk
    q_start = pl.multiple_of(i * tq, tq)
    q = q_ref[...]
    lane = lax.broadcasted_iota(jnp.int32, (1, LANES), 1)
    low = lane < HEAD_DIM
    row = lax.broadcasted_iota(jnp.int32, (tq, tk), 0)
    col = lax.broadcasted_iota(jnp.int32, (tq, tk), 1)
    zero = jnp.zeros((), _BF16)

    heads = []
    for hd in range(HEADS_PER_TILE):
        qm = jnp.where(low if hd == 0 else ~low, q, zero)
        c_ref0 = c_ref[0, hd:hd + 1, pl.ds(q_start, LANES)][:, 0:1]

        def step(kb, carry, masked, qm=qm, c_ref0=c_ref0, hd=hd):
            m, l, acc = carry
            ks = pl.multiple_of(kb * tk, tk)
            kt = k_ref[pl.ds(ks, tk), :]
            vt = v_ref[pl.ds(ks, tk), :]
            s = lax.dot_general(qm, kt, _NT, preferred_element_type=_F32)
            s = s + (c_ref0 - c_ref[0, hd:hd + 1, pl.ds(ks, tk)])
            if masked:
                s = jnp.where(col + (ks - q_start) <= row, s, MASK_VALUE)
            m_new = jnp.maximum(m, jnp.max(s, axis=-1, keepdims=True))
            p = jnp.exp(s - m_new)
            alpha = jnp.exp(m - m_new)
            l = alpha * l + jnp.sum(p, axis=-1, keepdims=True)
            acc = alpha * acc + jnp.dot(p.astype(_BF16), vt, preferred_element_type=_F32)
            return m_new, l, acc

        carry = (jnp.full((tq, 1), MASK_VALUE, _F32), jnp.zeros((tq, 1), _F32),
                 jnp.zeros((tq, LANES), _F32))
        carry = lax.fori_loop(0, i * steps_per_tile,
                              functools.partial(step, masked=False), carry)
        for d in range(steps_per_tile):
            carry = step(i * steps_per_tile + d, carry, masked=True)
        _, l, acc = carry
        heads.append(acc / l)

    o_ref[...] = jnp.where(low, heads[0], heads[1]).astype(_BF16)


def _fox_attention(qkv, c3, batch, seq_len):
    t = qkv.shape[0]
    hd = N_HEADS * HEAD_DIM
    n_pairs = N_HEADS // HEADS_PER_TILE
    nq = seq_len // FOX_TQ
    return pl.pallas_call(
        _fox_kernel,
        grid=(batch, n_pairs, nq),
        in_specs=[
            pl.BlockSpec((FOX_TQ, LANES), lambda b, p, i: (b * nq + i, p)),
            pl.BlockSpec((seq_len, LANES), lambda b, p, i: (b, n_pairs + p)),
            pl.BlockSpec((seq_len, LANES), lambda b, p, i: (b, 2 * n_pairs + p)),
            pl.BlockSpec((1, HEADS_PER_TILE, seq_len), lambda b, p, i: (p, 0, b)),
        ],
        out_specs=pl.BlockSpec((FOX_TQ, LANES), lambda b, p, i: (b * nq + i, p)),
        out_shape=jax.ShapeDtypeStruct((t, hd), _BF16),
        compiler_params=_params(("arbitrary", "arbitrary", "arbitrary")),
        name="fox_attn",
    )(qkv, qkv, qkv, c3)


def _out_kernel(a_ref, w_ref, x_ref, g1_ref, g2_ref, xo_ref, h_ref):
    y = jnp.dot(a_ref[...], w_ref[...], preferred_element_type=_F32)
    xn = x_ref[...] + y * _rms_scale(y) * g1_ref[...]
    xo_ref[...] = xn
    h_ref[...] = (xn * _rms_scale(xn) * g2_ref[...]).astype(_BF16)


def _out_proj(attn, w_bf16, x2, g_post, g_pre_ffn):
    t, d = x2.shape
    k = attn.shape[1]
    row = lambda i: (i, 0)
    fixed = lambda i: (0, 0)
    return pl.pallas_call(
        _out_kernel,
        grid=(t // OUT_TM,),
        in_specs=[
            pl.BlockSpec((OUT_TM, k), row),
            pl.BlockSpec((k, d), fixed),
            pl.BlockSpec((OUT_TM, d), row),
            pl.BlockSpec((1, d), fixed),
            pl.BlockSpec((1, d), fixed),
        ],
        out_specs=[pl.BlockSpec((OUT_TM, d), row), pl.BlockSpec((OUT_TM, d), row)],
        out_shape=[jax.ShapeDtypeStruct((t, d), _F32), jax.ShapeDtypeStruct((t, d), _BF16)],
        compiler_params=_params(("arbitrary",)),
        name="out_proj",
    )(attn, w_bf16, x2, g_post, g_pre_ffn)


def _ffn_kernel(h_ref, wg_ref, wu_ref, wd_ref, x_ref, g_ref, o_ref, acc_ref):
    f = pl.program_id(1)
    h = h_ref[...]
    gate = jnp.dot(h, wg_ref[...], preferred_element_type=_F32)
    up = jnp.dot(h, wu_ref[...], preferred_element_type=_F32)
    act = (gate * (1.0 / (1.0 + jnp.exp(-gate))) * up).astype(_BF16)
    part = jnp.dot(act, wd_ref[...], preferred_element_type=_F32)

    @pl.when(f == 0)
    def _():
        acc_ref[...] = part

    @pl.when(f > 0)
    def _():
        acc_ref[...] += part

    @pl.when(f == pl.num_programs(1) - 1)
    def _():
        y = acc_ref[...]
        o_ref[...] = x_ref[...] + y * _rms_scale(y) * g_ref[...]


def _ffn(h, w_gu_bf16, w_down_bf16, x2, g_post):
    t, d = x2.shape
    d_ff = w_down_bf16.shape[0]
    nf = d_ff // FFN_TF
    return pl.pallas_call(
        _ffn_kernel,
        grid=(t // FFN_TM, nf),
        in_specs=[
            pl.BlockSpec((FFN_TM, d), lambda i, f: (i, 0)),
            pl.BlockSpec((d, FFN_TF), lambda i, f: (0, f)),
            pl.BlockSpec((d, FFN_TF), lambda i, f: (0, nf + f)),
            pl.BlockSpec((FFN_TF, d), lambda i, f: (f, 0)),
            pl.BlockSpec((FFN_TM, d), lambda i, f: (i, 0)),
            pl.BlockSpec((1, d), lambda i, f: (0, 0)),
        ],
        out_specs=pl.BlockSpec((FFN_TM, d), lambda i, f: (i, 0)),
        out_shape=jax.ShapeDtypeStruct((t, d), _F32),
        scratch_shapes=[pltpu.VMEM((FFN_TM, d), _F32)],
        compiler_params=_params(("arbitrary", "arbitrary")),
        name="ffn",
    )(h, w_gu_bf16, w_gu_bf16, w_down_bf16, x2, g_post)


def _rotary_lane_tables(positions):
    half = ROT_DIM // 2
    inv_freq = ROPE_THETA ** (-jnp.arange(0, ROT_DIM, 2, dtype=_F32) / ROT_DIM)
    ang = positions.astype(_F32).reshape(-1, 1) * inv_freq
    cos, sin = jnp.cos(ang), jnp.sin(ang)
    t = ang.shape[0]
    pad = HEAD_DIM - ROT_DIM
    ra = jnp.concatenate([cos, cos, jnp.ones((t, pad), _F32)], axis=1)
    rb = jnp.concatenate([-sin, jnp.zeros((t, half + pad), _F32)], axis=1)
    rc = jnp.concatenate([jnp.zeros((t, half), _F32), sin, jnp.zeros((t, pad), _F32)], axis=1)
    tile = lambda a: jnp.tile(a, (1, HEADS_PER_TILE))
    return tile(ra), tile(rb), tile(rc)


def kernel(x, positions, norm_gains, swa_w_in, swa_sinks, swa_w_out, fox_w_in, fox_b_f,
           fox_w_out, ffn_w_gate_up, ffn_w_down):
    batch, seq_len, d = x.shape
    depth = norm_gains.shape[0]
    hd = N_HEADS * HEAD_DIM
    assert d == D_MODEL and seq_len % PROJ_TM == 0 and seq_len % FOX_TQ == 0
    assert FOX_TQ % FOX_TK == 0 and WINDOW <= BLOCK

    x2 = x.reshape(batch * seq_len, d)
    ra, rb, rc = _rotary_lane_tables(positions)
    gains = norm_gains.reshape(depth, 4, 1, d)

    for layer in range(depth):
        j = layer // 2
        g = gains[layer]
        if layer % 2 == 0:
            qkv = _proj_swa(x2, g[0], swa_w_in[j].astype(_BF16), ra, rb, rc)
            attn = _swa_attention(qkv, swa_sinks[j].astype(_F32), batch, seq_len)
            w_out = swa_w_out[j]
        else:
            w_in = fox_w_in[j]
            wf_pad = jnp.pad(w_in[:, 3 * hd:], ((0, 0), (0, LANES - N_HEADS)))
            qkv, c = _proj_fox(x2, g[0], w_in[:, :3 * hd].astype(_BF16), wf_pad,
                               fox_b_f[j].astype(_F32).reshape(N_HEADS, 1), seq_len)
            c3 = c.reshape(N_HEADS // HEADS_PER_TILE, HEADS_PER_TILE, batch * seq_len)
            attn = _fox_attention(qkv, c3, batch, seq_len)
            w_out = fox_w_out[j]
        x2, h = _out_proj(attn, w_out.astype(_BF16), x2, g[1], g[2])
        x2 = _ffn(h, ffn_w_gate_up[layer].astype(_BF16), ffn_w_down[layer].astype(_BF16),
                  x2, g[3])
    return x2.reshape(batch, seq_len, d)
```

```python
import functools

import jax
import jax.numpy as jnp
from jax import lax
from jax.experimental import pallas as pl
from jax.experimental.pallas import tpu as pltpu

D_MODEL = 2048
HEAD_DIM = 64
N_HEADS = D_MODEL // HEAD_DIM
SWA_KV_HEADS = 8
SWA_GROUP = N_HEADS // SWA_KV_HEADS
WINDOW = 128
BLOCK = 128
ROPE_THETA = 500000.0
ROT_DIM = HEAD_DIM // 4
RMS_EPS = 1e-6

LANES = 128
HEADS_PER_TILE = LANES // HEAD_DIM
VMEM_LIMIT_BYTES = 56 * 1024 * 1024
MASK_VALUE = -1e30

PROJ_TM = 1024
PROJ_TN = 512
OUT_TM = 512
FFN_TM = 512
FFN_TF = 512
FOX_T = 512
BIAS_TERMS = 3
LOG2E = 1.4426950408889634

_F32 = jnp.float32
_BF16 = jnp.bfloat16
_NT = (((1,), (1,)), ((), ()))


def _params(semantics):
    return pltpu.CompilerParams(dimension_semantics=semantics,
                                vmem_limit_bytes=VMEM_LIMIT_BYTES)


def _rms_scale(v):
    return lax.rsqrt(jnp.mean(v * v, axis=-1, keepdims=True) + RMS_EPS)


def _split3(v):
    hi = v.astype(_BF16)
    r1 = v - hi.astype(_F32)
    mid = r1.astype(_BF16)
    lo = (r1 - mid.astype(_F32)).astype(_BF16)
    return hi, mid, lo


def _proj_swa_kernel(x_ref, g_ref, w_ref, ra_ref, rb_ref, rc_ref, o_ref, h_ref,
                     *, n_q_blocks, n_rope_blocks):
    j = pl.program_id(1)

    @pl.when(j == 0)
    def _():
        x = x_ref[...]
        h_ref[...] = (x * _rms_scale(x) * g_ref[...]).astype(_BF16)

    res = jnp.dot(h_ref[...], w_ref[...], preferred_element_type=_F32)

    @pl.when(j < n_rope_blocks)
    def _():
        scale = jnp.where(j < n_q_blocks, HEAD_DIM ** -0.5, 1.0).astype(_F32)
        ra, rb, rc = ra_ref[...], rb_ref[...], rc_ref[...]
        half = ROT_DIM // 2
        for c in range(res.shape[1] // LANES):
            r = res[:, c * LANES:(c + 1) * LANES]
            rot = (r * ra + pltpu.roll(r, LANES - half, 1) * rb
                   + pltpu.roll(r, half, 1) * rc)
            o_ref[:, c * LANES:(c + 1) * LANES] = (rot * scale).astype(_BF16)

    @pl.when(j >= n_rope_blocks)
    def _():
        o_ref[...] = res.astype(_BF16)


def _proj_swa(x2, gain, w_bf16, ra, rb, rc):
    t, d = x2.shape
    n = w_bf16.shape[1]
    qd = N_HEADS * HEAD_DIM
    kd = SWA_KV_HEADS * HEAD_DIM
    kern = functools.partial(_proj_swa_kernel, n_q_blocks=qd // PROJ_TN,
                             n_rope_blocks=(qd + kd) // PROJ_TN)
    return pl.pallas_call(
        kern,
        grid=(t // PROJ_TM, n // PROJ_TN),
        in_specs=[
            pl.BlockSpec((PROJ_TM, d), lambda i, j: (i, 0)),
            pl.BlockSpec((1, d), lambda i, j: (0, 0)),
            pl.BlockSpec((d, PROJ_TN), lambda i, j: (0, j)),
            pl.BlockSpec((PROJ_TM, LANES), lambda i, j: (i, 0)),
            pl.BlockSpec((PROJ_TM, LANES), lambda i, j: (i, 0)),
            pl.BlockSpec((PROJ_TM, LANES), lambda i, j: (i, 0)),
        ],
        out_specs=pl.BlockSpec((PROJ_TM, PROJ_TN), lambda i, j: (i, j)),
        out_shape=jax.ShapeDtypeStruct((t, n), _BF16),
        scratch_shapes=[pltpu.VMEM((PROJ_TM, d), _BF16)],
        compiler_params=_params(("arbitrary", "arbitrary")),
        name="proj_swa",
    )(x2, gain, w_bf16, ra, rb, rc)


def _proj_fox_kernel(x_ref, g_ref, w_ref, wf_ref, bf_ref, tri_ref, sel_ref, o_ref, ka_ref,
                     h_ref, carry_ref, *, n_q_blocks, blocks_per_seq):
    i = pl.program_id(0)
    j = pl.program_id(1)

    @pl.when(j == 0)
    def _():
        x = x_ref[...]
        hf = x * _rms_scale(x) * g_ref[...]
        h_hi = hf.astype(_BF16)
        h_ref[...] = h_hi
        h_lo = (hf - h_hi.astype(_F32)).astype(_BF16)
        wf = wf_ref[...]
        w_hi = wf.astype(_BF16)
        w_lo = (wf - w_hi.astype(_F32)).astype(_BF16)
        logit = (jnp.dot(h_hi, w_hi, preferred_element_type=_F32)
                 + jnp.dot(h_lo, w_hi, preferred_element_type=_F32)
                 + jnp.dot(h_hi, w_lo, preferred_element_type=_F32))
        z = logit + bf_ref[...]
        log_f = jnp.minimum(z, 0.0) - jnp.log1p(jnp.exp(-jnp.abs(z)))
        tri = tri_ref[...]
        csum = sum(jnp.dot(tri, part, preferred_element_type=_F32)
                   for part in _split3(log_f))

        @pl.when(i % blocks_per_seq == 0)
        def _():
            carry_ref[...] = jnp.zeros_like(carry_ref)

        c = csum + carry_ref[...]
        carry_ref[...] = c[PROJ_TM - 1:PROJ_TM, :]
        ka = sum(jnp.dot(part, sel_ref[s], preferred_element_type=_F32)
                 for s, part in enumerate(_split3(c * -LOG2E)))
        ka_ref[...] = ka.astype(_BF16)

    res = jnp.dot(h_ref[...], w_ref[...], preferred_element_type=_F32)

    @pl.when(j < n_q_blocks)
    def _():
        o_ref[...] = (res * (HEAD_DIM ** -0.5 * LOG2E)).astype(_BF16)

    @pl.when(j >= n_q_blocks)
    def _():
        o_ref[...] = res.astype(_BF16)


def _bias_lane_selectors():
    n_pairs = N_HEADS // HEADS_PER_TILE
    s = jnp.arange(BIAS_TERMS)[:, None, None]
    h = jnp.arange(LANES)[None, :, None]
    col = jnp.arange(n_pairs * LANES)[None, None, :]
    target = (h // HEADS_PER_TILE) * LANES + BIAS_TERMS * (h % HEADS_PER_TILE) + s
    return ((col == target) & (h < N_HEADS)).astype(_BF16)


def _proj_fox(x2, gain, w_bf16, wf_pad, bf_pad, seq_len):
    t, d = x2.shape
    n = w_bf16.shape[1]
    ka_cols = (N_HEADS // HEADS_PER_TILE) * LANES
    tri = (lax.broadcasted_iota(jnp.int32, (PROJ_TM, PROJ_TM), 1)
           <= lax.broadcasted_iota(jnp.int32, (PROJ_TM, PROJ_TM), 0)).astype(_BF16)
    kern = functools.partial(_proj_fox_kernel,
                             n_q_blocks=(N_HEADS * HEAD_DIM) // PROJ_TN,
                             blocks_per_seq=seq_len // PROJ_TM)
    return pl.pallas_call(
        kern,
        grid=(t // PROJ_TM, n // PROJ_TN),
        in_specs=[
            pl.BlockSpec((PROJ_TM, d), lambda i, j: (i, 0)),
            pl.BlockSpec((1, d), lambda i, j: (0, 0)),
            pl.BlockSpec((d, PROJ_TN), lambda i, j: (0, j)),
            pl.BlockSpec((d, LANES), lambda i, j: (0, 0)),
            pl.BlockSpec((1, LANES), lambda i, j: (0, 0)),
            pl.BlockSpec((PROJ_TM, PROJ_TM), lambda i, j: (0, 0)),
            pl.BlockSpec((BIAS_TERMS, LANES, ka_cols), lambda i, j: (0, 0, 0)),
        ],
        out_specs=[
            pl.BlockSpec((PROJ_TM, PROJ_TN), lambda i, j: (i, j)),
            pl.BlockSpec((PROJ_TM, ka_cols), lambda i, j: (i, 0)),
        ],
        out_shape=[jax.ShapeDtypeStruct((t, n), _BF16),
                   jax.ShapeDtypeStruct((t, ka_cols), _BF16)],
        scratch_shapes=[pltpu.VMEM((PROJ_TM, d), _BF16),
                        pltpu.VMEM((1, LANES), _F32)],
        compiler_params=_params(("arbitrary", "arbitrary")),
        name="proj_fox",
    )(x2, gain, w_bf16, wf_pad, bf_pad, tri, _bias_lane_selectors())


def _swa_kernel(sink_ref, q_ref, kp_ref, kc_ref, vp_ref, vc_ref, o_ref):
    g = pl.program_id(1)
    n = pl.program_id(2)
    lane = lax.broadcasted_iota(jnp.int32, (1, LANES), 1)
    low = lane < HEAD_DIM

    def both_halves(prev_ref, cur_ref):
        band = jnp.concatenate([prev_ref[...], cur_ref[...]], axis=0).astype(_F32)
        swapped = pltpu.roll(band, HEAD_DIM, 1)
        return (jnp.where(low, band, swapped).astype(_BF16),
                jnp.where(low, swapped, band).astype(_BF16))

    k_dup = both_halves(kp_ref, kc_ref)
    v_dup = both_halves(vp_ref, vc_ref)

    q_pos = lax.broadcasted_iota(jnp.int32, (BLOCK, 2 * BLOCK), 0)
    s_pos = lax.broadcasted_iota(jnp.int32, (BLOCK, 2 * BLOCK), 1)
    rel = BLOCK + q_pos - s_pos
    valid = (rel >= 0) & (rel < WINDOW) & ((s_pos >= BLOCK) | (n > 0))

    zero = jnp.zeros((), _BF16)
    for kv in range(2):
        chunks = [q_ref[:, (2 * kv + c) * LANES:(2 * kv + c + 1) * LANES]
                  for c in range(SWA_GROUP // HEADS_PER_TILE)]
        q_stack = jnp.concatenate(
            [jnp.where(low if hh == 0 else ~low, qc, zero)
             for qc in chunks for hh in range(HEADS_PER_TILE)], axis=0)
        s_all = lax.dot_general(q_stack, k_dup[kv], _NT, preferred_element_type=_F32)
        probs, inv = [], []
        for h in range(SWA_GROUP):
            s = jnp.where(valid, s_all[h * BLOCK:(h + 1) * BLOCK], MASK_VALUE)
            sink = sink_ref[g * 2 * SWA_GROUP + kv * SWA_GROUP + h]
            m = jnp.maximum(jnp.max(s, axis=-1, keepdims=True), sink)
            p = jnp.exp(s - m)
            denom = jnp.sum(p, axis=-1, keepdims=True) + jnp.exp(sink - m)
            probs.append(p.astype(_BF16))
            inv.append(1.0 / denom)
        o_all = jnp.dot(jnp.concatenate(probs, axis=0), v_dup[kv],
                        preferred_element_type=_F32)
        for c in range(SWA_GROUP // HEADS_PER_TILE):
            h0, h1 = HEADS_PER_TILE * c, HEADS_PER_TILE * c + 1
            o0 = o_all[h0 * BLOCK:(h0 + 1) * BLOCK] * inv[h0]
            o1 = o_all[h1 * BLOCK:(h1 + 1) * BLOCK] * inv[h1]
            col = (2 * kv + c) * LANES
            o_ref[:, col:col + LANES] = jnp.where(low, o0, o1).astype(_BF16)


def _swa_attention(qkv, sinks, batch, seq_len):
    t = qkv.shape[0]
    nblk = seq_len // BLOCK
    qd = N_HEADS * HEAD_DIM
    kd = SWA_KV_HEADS * HEAD_DIM
    q_tile = 2 * SWA_GROUP * HEAD_DIM
    k_col0 = qd // LANES
    v_col0 = (qd + kd) // LANES

    def cur(col0):
        return lambda b, g, n: (b * nblk + n, col0 + g)

    def prev(col0):
        return lambda b, g, n: (b * nblk + jnp.maximum(n - 1, 0), col0 + g)

    return pl.pallas_call(
        _swa_kernel,
        grid=(batch, SWA_KV_HEADS // 2, nblk),
        in_specs=[
            pl.BlockSpec(memory_space=pltpu.SMEM),
            pl.BlockSpec((BLOCK, q_tile), lambda b, g, n: (b * nblk + n, g)),
            pl.BlockSpec((BLOCK, LANES), prev(k_col0)),
            pl.BlockSpec((BLOCK, LANES), cur(k_col0)),
            pl.BlockSpec((BLOCK, LANES), prev(v_col0)),
            pl.BlockSpec((BLOCK, LANES), cur(v_col0)),
        ],
        out_specs=pl.BlockSpec((BLOCK, q_tile), lambda b, g, n: (b * nblk + n, g)),
        out_shape=jax.ShapeDtypeStruct((t, qd), _BF16),
        compiler_params=_params(("arbitrary", "arbitrary", "arbitrary")),
        name="swa_attn",
    )(sinks, qkv, qkv, qkv, qkv, qkv)


def _fox_kernel(q_ref, k_ref, ka_ref, v_ref, o_ref, qa_ref, s_ref, m_ref, acc_ref):
    i = pl.program_id(2)
    t = FOX_T
    q = q_ref[...]
    lane = lax.broadcasted_iota(jnp.int32, (1, LANES), 1)
    low = lane < HEAD_DIM
    zero = jnp.zeros((), _BF16)

    for hd in range(HEADS_PER_TILE):
        own = low if hd == 0 else ~low
        bias_lanes = (lane >= BIAS_TERMS * hd) & (lane < BIAS_TERMS * (hd + 1))
        pick = jnp.broadcast_to(jnp.where(bias_lanes, 1.0, 0.0).astype(_BF16), (t, LANES))
        qa_ref[hd] = jnp.concatenate([jnp.where(own, q, zero), pick], axis=1)
        m_ref[hd] = jnp.full((t, 1), MASK_VALUE, _F32)
        acc_ref[hd] = jnp.zeros((t, 2 * LANES), _F32)

    def scores(kb, slot):
        ks = pl.multiple_of(kb * t, t)
        kt = jnp.concatenate([k_ref[pl.ds(ks, t), :], ka_ref[pl.ds(ks, t), :]], axis=1)
        for hd in range(HEADS_PER_TILE):
            s_ref[slot, hd] = lax.dot_general(qa_ref[hd], kt, _NT, preferred_element_type=_F32)

    def absorb(kb, slot, masked):
        ks = pl.multiple_of(kb * t, t)
        vt = jnp.concatenate([v_ref[pl.ds(ks, t), :], jnp.ones((t, LANES), _BF16)], axis=1)
        for hd in range(HEADS_PER_TILE):
            s = s_ref[slot, hd]
            if masked:
                causal = (lax.broadcasted_iota(jnp.int32, (t, t), 1)
                          <= lax.broadcasted_iota(jnp.int32, (t, t), 0))
                s = jnp.where(causal, s, MASK_VALUE)
            m = m_ref[hd]
            m_new = jnp.maximum(m, jnp.max(s, axis=-1, keepdims=True))
            p = jnp.exp2(s - m_new).astype(_BF16)
            acc_ref[hd] = (jnp.exp2(m - m_new) * acc_ref[hd]
                           + jnp.dot(p, vt, preferred_element_type=_F32))
            m_ref[hd] = m_new

    scores(0, 0)

    def two_blocks(j, carry):
        kb = 2 * j
        scores(kb + 1, 1)
        absorb(kb, 0, False)
        scores(kb + 2, 0)
        absorb(kb + 1, 1, False)
        return carry

    lax.fori_loop(0, lax.shift_right_logical(i, 1), two_blocks, 0)

    @pl.when((i & 1) == 1)
    def _():
        scores(i, 1)
        absorb(i - 1, 0, False)
        absorb(i, 1, True)

    @pl.when((i & 1) == 0)
    def _():
        absorb(i, 0, True)

    outs = [acc_ref[hd][:, :LANES] / acc_ref[hd][:, LANES:] for hd in range(HEADS_PER_TILE)]
    o_ref[...] = jnp.where(low, outs[0], outs[1]).astype(_BF16)


def _fox_attention(qkv, ka, batch, seq_len):
    t = qkv.shape[0]
    hd = N_HEADS * HEAD_DIM
    n_pairs = N_HEADS // HEADS_PER_TILE
    nq = seq_len // FOX_T
    return pl.pallas_call(
        _fox_kernel,
        grid=(batch, n_pairs, nq),
        in_specs=[
            pl.BlockSpec((FOX_T, LANES), lambda b, p, i: (b * nq + i, p)),
            pl.BlockSpec((seq_len, LANES), lambda b, p, i: (b, n_pairs + p)),
            pl.BlockSpec((seq_len, LANES), lambda b, p, i: (b, p)),
            pl.BlockSpec((seq_len, LANES), lambda b, p, i: (b, 2 * n_pairs + p)),
        ],
        out_specs=pl.BlockSpec((FOX_T, LANES), lambda b, p, i: (b * nq + i, p)),
        out_shape=jax.ShapeDtypeStruct((t, hd), _BF16),
        scratch_shapes=[
            pltpu.VMEM((HEADS_PER_TILE, FOX_T, 2 * LANES), _BF16),
            pltpu.VMEM((2, HEADS_PER_TILE, FOX_T, FOX_T), _F32),
            pltpu.VMEM((HEADS_PER_TILE, FOX_T, 1), _F32),
            pltpu.VMEM((HEADS_PER_TILE, FOX_T, 2 * LANES), _F32),
        ],
        compiler_params=_params(("arbitrary", "arbitrary", "arbitrary")),
        name="fox_attn",
    )(qkv, qkv, ka, qkv)


def _out_kernel(a_ref, w_ref, x_ref, g1_ref, g2_ref, xo_ref, h_ref):
    y = jnp.dot(a_ref[...], w_ref[...], preferred_element_type=_F32)
    xn = x_ref[...] + y * _rms_scale(y) * g1_ref[...]
    xo_ref[...] = xn
    h_ref[...] = (xn * _rms_scale(xn) * g2_ref[...]).astype(_BF16)


def _out_proj(attn, w_bf16, x2, g_post, g_pre_ffn):
    t, d = x2.shape
    k = attn.shape[1]
    row = lambda i: (i, 0)
    fixed = lambda i: (0, 0)
    return pl.pallas_call(
        _out_kernel,
        grid=(t // OUT_TM,),
        in_specs=[
            pl.BlockSpec((OUT_TM, k), row),
            pl.BlockSpec((k, d), fixed),
            pl.BlockSpec((OUT_TM, d), row),
            pl.BlockSpec((1, d), fixed),
            pl.BlockSpec((1, d), fixed),
        ],
        out_specs=[pl.BlockSpec((OUT_TM, d), row), pl.BlockSpec((OUT_TM, d), row)],
        out_shape=[jax.ShapeDtypeStruct((t, d), _F32), jax.ShapeDtypeStruct((t, d), _BF16)],
        compiler_params=_params(("arbitrary",)),
        name="out_proj",
    )(attn, w_bf16, x2, g_post, g_pre_ffn)


def _ffn_kernel(h_ref, wg_ref, wu_ref, wd_ref, x_ref, g_ref, o_ref, acc_ref):
    f = pl.program_id(1)
    h = h_ref[...]
    gate = jnp.dot(h, wg_ref[...], preferred_element_type=_F32)
    up = jnp.dot(h, wu_ref[...], preferred_element_type=_F32)
    act = (gate * (1.0 / (1.0 + jnp.exp(-gate))) * up).astype(_BF16)
    part = jnp.dot(act, wd_ref[...], preferred_element_type=_F32)

    @pl.when(f == 0)
    def _():
        acc_ref[...] = part

    @pl.when(f > 0)
    def _():
        acc_ref[...] += part

    @pl.when(f == pl.num_programs(1) - 1)
    def _():
        y = acc_ref[...]
        o_ref[...] = x_ref[...] + y * _rms_scale(y) * g_ref[...]


def _ffn(h, w_gu_bf16, w_down_bf16, x2, g_post):
    t, d = x2.shape
    d_ff = w_down_bf16.shape[0]
    nf = d_ff // FFN_TF
    return pl.pallas_call(
        _ffn_kernel,
        grid=(t // FFN_TM, nf),
        in_specs=[
            pl.BlockSpec((FFN_TM, d), lambda i, f: (i, 0)),
            pl.BlockSpec((d, FFN_TF), lambda i, f: (0, f)),
            pl.BlockSpec((d, FFN_TF), lambda i, f: (0, nf + f)),
            pl.BlockSpec((FFN_TF, d), lambda i, f: (f, 0)),
            pl.BlockSpec((FFN_TM, d), lambda i, f: (i, 0)),
            pl.BlockSpec((1, d), lambda i, f: (0, 0)),
        ],
        out_specs=pl.BlockSpec((FFN_TM, d), lambda i, f: (i, 0)),
        out_shape=jax.ShapeDtypeStruct((t, d), _F32),
        scratch_shapes=[pltpu.VMEM((FFN_TM, d), _F32)],
        compiler_params=_params(("arbitrary", "arbitrary")),
        name="ffn",
    )(h, w_gu_bf16, w_gu_bf16, w_down_bf16, x2, g_post)


def _rotary_lane_tables(positions):
    half = ROT_DIM // 2
    inv_freq = ROPE_THETA ** (-jnp.arange(0, ROT_DIM, 2, dtype=_F32) / ROT_DIM)
    ang = positions.astype(_F32).reshape(-1, 1) * inv_freq
    cos, sin = jnp.cos(ang), jnp.sin(ang)
    t = ang.shape[0]
    pad = HEAD_DIM - ROT_DIM
    ra = jnp.concatenate([cos, cos, jnp.ones((t, pad), _F32)], axis=1)
    rb = jnp.concatenate([-sin, jnp.zeros((t, half + pad), _F32)], axis=1)
    rc = jnp.concatenate([jnp.zeros((t, half), _F32), sin, jnp.zeros((t, pad), _F32)], axis=1)
    tile = lambda a: jnp.tile(a, (1, HEADS_PER_TILE))
    return tile(ra), tile(rb), tile(rc)


def kernel(x, positions, norm_gains, swa_w_in, swa_sinks, swa_w_out, fox_w_in, fox_b_f,
           fox_w_out, ffn_w_gate_up, ffn_w_down):
    batch, seq_len, d = x.shape
    depth = norm_gains.shape[0]
    hd = N_HEADS * HEAD_DIM
    assert d == D_MODEL and seq_len % PROJ_TM == 0 and seq_len % FOX_T == 0
    assert WINDOW <= BLOCK

    x2 = x.reshape(batch * seq_len, d)
    ra, rb, rc = _rotary_lane_tables(positions)
    gains = norm_gains.reshape(depth, 4, 1, d)

    for layer in range(depth):
        j = layer // 2
        g = gains[layer]
        if layer % 2 == 0:
            qkv = _proj_swa(x2, g[0], swa_w_in[j].astype(_BF16), ra, rb, rc)
            attn = _swa_attention(qkv, swa_sinks[j].astype(_F32), batch, seq_len)
            w_out = swa_w_out[j]
        else:
            w_in = fox_w_in[j]
            wf_pad = jnp.pad(w_in[:, 3 * hd:], ((0, 0), (0, LANES - N_HEADS)))
            bf_pad = jnp.pad(fox_b_f[j].astype(_F32), (0, LANES - N_HEADS)).reshape(1, LANES)
            qkv, ka = _proj_fox(x2, g[0], w_in[:, :3 * hd].astype(_BF16), wf_pad, bf_pad, seq_len)
            attn = _fox_attention(qkv, ka, batch, seq_len)
            w_out = fox_w_out[j]
        x2, h = _out_proj(attn, w_out.astype(_BF16), x2, g[1], g[2])
        x2 = _ffn(h, ffn_w_gate_up[layer].astype(_BF16), ffn_w_down[layer].astype(_BF16),
                  x2, g[3])
    return x2.reshape(batch, seq_len, d)
```

```python
import functools

import jax
import jax.numpy as jnp
from jax import lax
from jax.experimental import pallas as pl
from jax.experimental.pallas import tpu as pltpu

D_MODEL = 2048
HEAD_DIM = 64
N_HEADS = D_MODEL // HEAD_DIM
SWA_KV_HEADS = 8
SWA_GROUP = N_HEADS // SWA_KV_HEADS
WINDOW = 128
BLOCK = 128
ROPE_THETA = 500000.0
ROT_DIM = HEAD_DIM // 4
RMS_EPS = 1e-6

LANES = 128
HEADS_PER_TILE = LANES // HEAD_DIM
VMEM_LIMIT_BYTES = 56 * 1024 * 1024
MASK_VALUE = -1e30

PROJ_TM = 1024
PROJ_TN = 512
OUT_TM = 512
FFN_TM = 512
FFN_TF = 512
FOX_T = 512
BIAS_TERMS = 3
LOG2E = 1.4426950408889634

_F32 = jnp.float32
_BF16 = jnp.bfloat16
_NT = (((1,), (1,)), ((), ()))


def _params(semantics):
    return pltpu.CompilerParams(dimension_semantics=semantics,
                                vmem_limit_bytes=VMEM_LIMIT_BYTES)


def _rms_scale(v):
    return lax.rsqrt(jnp.mean(v * v, axis=-1, keepdims=True) + RMS_EPS)


def _proj_swa_kernel(x_ref, g_ref, w_ref, ra_ref, rb_ref, rc_ref, o_ref, h_ref,
                     *, n_q_blocks, n_rope_blocks):
    j = pl.program_id(1)

    @pl.when(j == 0)
    def _():
        x = x_ref[...]
        h_ref[...] = (x * _rms_scale(x) * g_ref[...]).astype(_BF16)

    res = jnp.dot(h_ref[...], w_ref[...], preferred_element_type=_F32)

    rope = (j < n_rope_blocks).astype(_F32)
    scale = jnp.where(j < n_q_blocks, HEAD_DIM ** -0.5, 1.0).astype(_F32)
    ra = (ra_ref[...] * rope + (1.0 - rope)) * scale
    rb = rb_ref[...] * (rope * scale)
    rc = rc_ref[...] * (rope * scale)
    half = ROT_DIM // 2
    for c in range(res.shape[1] // LANES):
        r = res[:, c * LANES:(c + 1) * LANES]
        rot = r * ra + pltpu.roll(r, LANES - half, 1) * rb + pltpu.roll(r, half, 1) * rc
        o_ref[:, c * LANES:(c + 1) * LANES] = rot.astype(_BF16)


def _proj_swa(x2, gain, w_bf16, ra, rb, rc):
    t, d = x2.shape
    n = w_bf16.shape[1]
    qd = N_HEADS * HEAD_DIM
    kd = SWA_KV_HEADS * HEAD_DIM
    kern = functools.partial(_proj_swa_kernel, n_q_blocks=qd // PROJ_TN,
                             n_rope_blocks=(qd + kd) // PROJ_TN)
    return pl.pallas_call(
        kern,
        grid=(t // PROJ_TM, n // PROJ_TN),
        in_specs=[
            pl.BlockSpec((PROJ_TM, d), lambda i, j: (i, 0)),
            pl.BlockSpec((1, d), lambda i, j: (0, 0)),
            pl.BlockSpec((d, PROJ_TN), lambda i, j: (0, j)),
            pl.BlockSpec((PROJ_TM, LANES), lambda i, j: (i, 0)),
            pl.BlockSpec((PROJ_TM, LANES), lambda i, j: (i, 0)),
            pl.BlockSpec((PROJ_TM, LANES), lambda i, j: (i, 0)),
        ],
        out_specs=pl.BlockSpec((PROJ_TM, PROJ_TN), lambda i, j: (i, j)),
        out_shape=jax.ShapeDtypeStruct((t, n), _BF16),
        scratch_shapes=[pltpu.VMEM((PROJ_TM, d), _BF16)],
        compiler_params=_params(("arbitrary", "arbitrary")),
        name="proj_swa",
    )(x2, gain, w_bf16, ra, rb, rc)


def _pack_terms(v, lane):
    t0 = v.astype(_BF16).astype(_F32)
    r1 = v - t0
    t1 = r1.astype(_BF16).astype(_F32)
    t2 = (r1 - t1).astype(_BF16).astype(_F32)
    packed = jnp.where(lane < N_HEADS, t0,
                       jnp.where(lane < 2 * N_HEADS, pltpu.roll(t1, N_HEADS, 1),
                                 jnp.where(lane < 3 * N_HEADS, pltpu.roll(t2, 2 * N_HEADS, 1), 0.0)))
    return packed.astype(_BF16)


def _proj_fox_kernel(x_ref, g_ref, w_ref, wf_ref, bf_ref, tri_ref, sel_ref, o_ref, ka_ref,
                     h_ref, carry_ref, *, n_q_blocks, blocks_per_seq):
    i = pl.program_id(0)
    j = pl.program_id(1)

    @pl.when(j == 0)
    def _():
        x = x_ref[...]
        hf = x * _rms_scale(x) * g_ref[...]
        h_hi = hf.astype(_BF16)
        h_ref[...] = h_hi
        h_lo = (hf - h_hi.astype(_F32)).astype(_BF16)
        both = jnp.dot(h_hi, wf_ref[...], preferred_element_type=_F32)
        logit = (both[:, :LANES] + both[:, LANES:]
                 + jnp.dot(h_lo, wf_ref[:, :LANES], preferred_element_type=_F32))
        z = logit + bf_ref[...]
        log_f = jnp.minimum(z, 0.0) - jnp.log1p(jnp.exp(-jnp.abs(z)))
        lane = lax.broadcasted_iota(jnp.int32, log_f.shape, 1)
        part = jnp.dot(tri_ref[...], _pack_terms(log_f, lane), preferred_element_type=_F32)
        csum = (part + pltpu.roll(part, LANES - N_HEADS, 1)
                + pltpu.roll(part, LANES - 2 * N_HEADS, 1))

        @pl.when(i % blocks_per_seq == 0)
        def _():
            carry_ref[...] = jnp.zeros_like(carry_ref)

        c = csum + carry_ref[...]
        carry_ref[...] = c[PROJ_TM - 1:PROJ_TM, :]
        ka = jnp.dot(_pack_terms(c * -LOG2E, lane), sel_ref[...], preferred_element_type=_F32)
        ka_ref[...] = ka.astype(_BF16)

    res = jnp.dot(h_ref[...], w_ref[...], preferred_element_type=_F32)
    scale = jnp.where(j < n_q_blocks, HEAD_DIM ** -0.5 * LOG2E, 1.0).astype(_F32)
    o_ref[...] = (res * scale).astype(_BF16)


def _bias_lane_selector():
    n_pairs = N_HEADS // HEADS_PER_TILE
    row = jnp.arange(LANES)[:, None]
    s, h = row // N_HEADS, row % N_HEADS
    col = jnp.arange(n_pairs * LANES)[None, :]
    target = (h // HEADS_PER_TILE) * LANES + BIAS_TERMS * (h % HEADS_PER_TILE) + s
    return ((col == target) & (s < BIAS_TERMS)).astype(_BF16)


def _proj_fox(x2, gain, w_bf16, wf_hi_lo, bf_pad, seq_len):
    t, d = x2.shape
    n = 3 * N_HEADS * HEAD_DIM
    ka_cols = (N_HEADS // HEADS_PER_TILE) * LANES
    tri = (lax.broadcasted_iota(jnp.int32, (PROJ_TM, PROJ_TM), 1)
           <= lax.broadcasted_iota(jnp.int32, (PROJ_TM, PROJ_TM), 0)).astype(_BF16)
    kern = functools.partial(_proj_fox_kernel,
                             n_q_blocks=(N_HEADS * HEAD_DIM) // PROJ_TN,
                             blocks_per_seq=seq_len // PROJ_TM)
    return pl.pallas_call(
        kern,
        grid=(t // PROJ_TM, n // PROJ_TN),
        in_specs=[
            pl.BlockSpec((PROJ_TM, d), lambda i, j: (i, 0)),
            pl.BlockSpec((1, d), lambda i, j: (0, 0)),
            pl.BlockSpec((d, PROJ_TN), lambda i, j: (0, j)),
            pl.BlockSpec((d, 2 * LANES), lambda i, j: (0, 0)),
            pl.BlockSpec((1, LANES), lambda i, j: (0, 0)),
            pl.BlockSpec((PROJ_TM, PROJ_TM), lambda i, j: (0, 0)),
            pl.BlockSpec((LANES, ka_cols), lambda i, j: (0, 0)),
        ],
        out_specs=[
            pl.BlockSpec((PROJ_TM, PROJ_TN), lambda i, j: (i, j)),
            pl.BlockSpec((PROJ_TM, ka_cols), lambda i, j: (i, 0)),
        ],
        out_shape=[jax.ShapeDtypeStruct((t, n), _BF16),
                   jax.ShapeDtypeStruct((t, ka_cols), _BF16)],
        scratch_shapes=[pltpu.VMEM((PROJ_TM, d), _BF16),
                        pltpu.VMEM((1, LANES), _F32)],
        compiler_params=_params(("arbitrary", "arbitrary")),
        name="proj_fox",
    )(x2, gain, w_bf16, wf_hi_lo, bf_pad, tri, _bias_lane_selector())


def _swa_kernel(sink_ref, q_ref, kp_ref, kc_ref, vp_ref, vc_ref, o_ref):
    g = pl.program_id(1)
    n = pl.program_id(2)
    lane = lax.broadcasted_iota(jnp.int32, (1, LANES), 1)
    low = lane < HEAD_DIM

    def both_halves(prev_ref, cur_ref):
        band = jnp.concatenate([prev_ref[...], cur_ref[...]], axis=0).astype(_F32)
        swapped = pltpu.roll(band, HEAD_DIM, 1)
        return (jnp.where(low, band, swapped).astype(_BF16),
                jnp.where(low, swapped, band).astype(_BF16))

    k_dup = both_halves(kp_ref, kc_ref)
    v_dup = both_halves(vp_ref, vc_ref)

    q_pos = lax.broadcasted_iota(jnp.int32, (BLOCK, 2 * BLOCK), 0)
    s_pos = lax.broadcasted_iota(jnp.int32, (BLOCK, 2 * BLOCK), 1)
    rel = BLOCK + q_pos - s_pos
    valid = (rel >= 0) & (rel < WINDOW) & ((s_pos >= BLOCK) | (n > 0))

    zero = jnp.zeros((), _BF16)
    for kv in range(2):
        chunks = [q_ref[:, (2 * kv + c) * LANES:(2 * kv + c + 1) * LANES]
                  for c in range(SWA_GROUP // HEADS_PER_TILE)]
        q_stack = jnp.concatenate(
            [jnp.where(low if hh == 0 else ~low, qc, zero)
             for qc in chunks for hh in range(HEADS_PER_TILE)], axis=0)
        s_all = lax.dot_general(q_stack, k_dup[kv], _NT, preferred_element_type=_F32)
        probs, inv = [], []
        for h in range(SWA_GROUP):
            s = jnp.where(valid, s_all[h * BLOCK:(h + 1) * BLOCK], MASK_VALUE)
            sink = sink_ref[g * 2 * SWA_GROUP + kv * SWA_GROUP + h]
            m = jnp.maximum(jnp.max(s, axis=-1, keepdims=True), sink)
            p = jnp.exp(s - m)
            denom = jnp.sum(p, axis=-1, keepdims=True) + jnp.exp(sink - m)
            probs.append(p.astype(_BF16))
            inv.append(1.0 / denom)
        o_all = jnp.dot(jnp.concatenate(probs, axis=0), v_dup[kv],
                        preferred_element_type=_F32)
        for c in range(SWA_GROUP // HEADS_PER_TILE):
            h0, h1 = HEADS_PER_TILE * c, HEADS_PER_TILE * c + 1
            o0 = o_all[h0 * BLOCK:(h0 + 1) * BLOCK] * inv[h0]
            o1 = o_all[h1 * BLOCK:(h1 + 1) * BLOCK] * inv[h1]
            col = (2 * kv + c) * LANES
            o_ref[:, col:col + LANES] = jnp.where(low, o0, o1).astype(_BF16)


def _swa_attention(qkv, sinks, batch, seq_len):
    t = qkv.shape[0]
    nblk = seq_len // BLOCK
    qd = N_HEADS * HEAD_DIM
    kd = SWA_KV_HEADS * HEAD_DIM
    q_tile = 2 * SWA_GROUP * HEAD_DIM
    k_col0 = qd // LANES
    v_col0 = (qd + kd) // LANES

    def cur(col0):
        return lambda b, g, n: (b * nblk + n, col0 + g)

    def prev(col0):
        return lambda b, g, n: (b * nblk + jnp.maximum(n - 1, 0), col0 + g)

    return pl.pallas_call(
        _swa_kernel,
        grid=(batch, SWA_KV_HEADS // 2, nblk),
        in_specs=[
            pl.BlockSpec(memory_space=pltpu.SMEM),
            pl.BlockSpec((BLOCK, q_tile), lambda b, g, n: (b * nblk + n, g)),
            pl.BlockSpec((BLOCK, LANES), prev(k_col0)),
            pl.BlockSpec((BLOCK, LANES), cur(k_col0)),
            pl.BlockSpec((BLOCK, LANES), prev(v_col0)),
            pl.BlockSpec((BLOCK, LANES), cur(v_col0)),
        ],
        out_specs=pl.BlockSpec((BLOCK, q_tile), lambda b, g, n: (b * nblk + n, g)),
        out_shape=jax.ShapeDtypeStruct((t, qd), _BF16),
        compiler_params=_params(("arbitrary", "arbitrary", "arbitrary")),
        name="swa_attn",
    )(sinks, qkv, qkv, qkv, qkv, qkv)


def _fox_kernel(q_ref, k_ref, ka_ref, v_ref, o_ref, qa_ref, s_ref, m_ref, acc_ref):
    i = pl.program_id(2)
    t = FOX_T
    q = q_ref[...]
    lane = lax.broadcasted_iota(jnp.int32, (1, LANES), 1)
    low = lane < HEAD_DIM
    zero = jnp.zeros((), _BF16)

    for hd in range(HEADS_PER_TILE):
        own = low if hd == 0 else ~low
        bias_lanes = (lane >= BIAS_TERMS * hd) & (lane < BIAS_TERMS * (hd + 1))
        pick = jnp.broadcast_to(jnp.where(bias_lanes, 1.0, 0.0).astype(_BF16), (t, LANES))
        qa_ref[hd] = jnp.concatenate([jnp.where(own, q, zero), pick], axis=1)
        m_ref[hd] = jnp.full((t, LANES), MASK_VALUE, _F32)
        acc_ref[hd] = jnp.zeros((t, 2 * LANES), _F32)

    def scores(kb, slot):
        ks = pl.multiple_of(kb * t, t)
        kt = jnp.concatenate([k_ref[pl.ds(ks, t), :], ka_ref[pl.ds(ks, t), :]], axis=1)
        for hd in range(HEADS_PER_TILE):
            s_ref[slot, hd] = lax.dot_general(qa_ref[hd], kt, _NT, preferred_element_type=_F32)

    def absorb(kb, slot, masked):
        ks = pl.multiple_of(kb * t, t)
        vt = jnp.concatenate([v_ref[pl.ds(ks, t), :], jnp.ones((t, LANES), _BF16)], axis=1)
        for hd in range(HEADS_PER_TILE):
            s = s_ref[slot, hd]
            if masked:
                causal = (lax.broadcasted_iota(jnp.int32, (t, t), 1)
                          <= lax.broadcasted_iota(jnp.int32, (t, t), 0))
                s = jnp.where(causal, s, MASK_VALUE)
            m = m_ref[hd]
            m_new = jnp.maximum(m, jnp.max(s, axis=-1, keepdims=True))
            p = jnp.exp2(s - jnp.concatenate([m_new] * (t // LANES), axis=1)).astype(_BF16)
            alpha = jnp.exp2(m - m_new)
            acc_ref[hd] = (jnp.concatenate([alpha, alpha], axis=1) * acc_ref[hd]
                           + jnp.dot(p, vt, preferred_element_type=_F32))
            m_ref[hd] = m_new

    scores(0, 0)

    def two_blocks(j, carry):
        kb = 2 * j
        scores(kb + 1, 1)
        absorb(kb, 0, False)
        scores(kb + 2, 0)
        absorb(kb + 1, 1, False)
        return carry

    lax.fori_loop(0, lax.shift_right_logical(i, 1), two_blocks, 0)

    @pl.when((i & 1) == 1)
    def _():
        scores(i, 1)
        absorb(i - 1, 0, False)
        absorb(i, 1, True)

    @pl.when((i & 1) == 0)
    def _():
        absorb(i, 0, True)

    outs = [acc_ref[hd][:, :LANES] / acc_ref[hd][:, LANES:] for hd in range(HEADS_PER_TILE)]
    o_ref[...] = jnp.where(low, outs[0], outs[1]).astype(_BF16)


def _fox_attention(qkv, ka, batch, seq_len):
    t = qkv.shape[0]
    hd = N_HEADS * HEAD_DIM
    n_pairs = N_HEADS // HEADS_PER_TILE
    nq = seq_len // FOX_T
    return pl.pallas_call(
        _fox_kernel,
        grid=(batch, n_pairs, nq),
        in_specs=[
            pl.BlockSpec((FOX_T, LANES), lambda b, p, i: (b * nq + i, p)),
            pl.BlockSpec((seq_len, LANES), lambda b, p, i: (b, n_pairs + p)),
            pl.BlockSpec((seq_len, LANES), lambda b, p, i: (b, p)),
            pl.BlockSpec((seq_len, LANES), lambda b, p, i: (b, 2 * n_pairs + p)),
        ],
        out_specs=pl.BlockSpec((FOX_T, LANES), lambda b, p, i: (b * nq + i, p)),
        out_shape=jax.ShapeDtypeStruct((t, hd), _BF16),
        scratch_shapes=[
            pltpu.VMEM((HEADS_PER_TILE, FOX_T, 2 * LANES), _BF16),
            pltpu.VMEM((2, HEADS_PER_TILE, FOX_T, FOX_T), _F32),
            pltpu.VMEM((HEADS_PER_TILE, FOX_T, LANES), _F32),
            pltpu.VMEM((HEADS_PER_TILE, FOX_T, 2 * LANES), _F32),
        ],
        compiler_params=_params(("arbitrary", "arbitrary", "arbitrary")),
        name="fox_attn",
    )(qkv, qkv, ka, qkv)


def _out_kernel(a_ref, w_ref, x_ref, g1_ref, g2_ref, xo_ref, h_ref):
    y = jnp.dot(a_ref[...], w_ref[...], preferred_element_type=_F32)
    xn = x_ref[...] + y * _rms_scale(y) * g1_ref[...]
    xo_ref[...] = xn
    h_ref[...] = (xn * _rms_scale(xn) * g2_ref[...]).astype(_BF16)


def _out_proj(attn, w_bf16, x2, g_post, g_pre_ffn):
    t, d = x2.shape
    k = attn.shape[1]
    row = lambda i: (i, 0)
    fixed = lambda i: (0, 0)
    return pl.pallas_call(
        _out_kernel,
        grid=(t // OUT_TM,),
        in_specs=[
            pl.BlockSpec((OUT_TM, k), row),
            pl.BlockSpec((k, d), fixed),
            pl.BlockSpec((OUT_TM, d), row),
            pl.BlockSpec((1, d), fixed),
            pl.BlockSpec((1, d), fixed),
        ],
        out_specs=[pl.BlockSpec((OUT_TM, d), row), pl.BlockSpec((OUT_TM, d), row)],
        out_shape=[jax.ShapeDtypeStruct((t, d), _F32), jax.ShapeDtypeStruct((t, d), _BF16)],
        compiler_params=_params(("arbitrary",)),
        name="out_proj",
    )(attn, w_bf16, x2, g_post, g_pre_ffn)


def _ffn_kernel(h_ref, wg_ref, wu_ref, wd_ref, x_ref, g_ref, o_ref, acc_ref):
    f = pl.program_id(1)

    @pl.when(f == 0)
    def _():
        acc_ref[...] = jnp.zeros_like(acc_ref)

    h = h_ref[...]
    gate = jnp.dot(h, wg_ref[...], preferred_element_type=_F32)
    up = jnp.dot(h, wu_ref[...], preferred_element_type=_F32)
    act = (gate * (1.0 / (1.0 + jnp.exp(-gate))) * up).astype(_BF16)
    acc_ref[...] += jnp.dot(act, wd_ref[...], preferred_element_type=_F32)

    @pl.when(f == pl.num_programs(1) - 1)
    def _():
        y = acc_ref[...]
        o_ref[...] = x_ref[...] + y * _rms_scale(y) * g_ref[...]


def _ffn(h, w_gu_bf16, w_down_bf16, x2, g_post):
    t, d = x2.shape
    d_ff = w_down_bf16.shape[0]
    nf = d_ff // FFN_TF
    return pl.pallas_call(
        _ffn_kernel,
        grid=(t // FFN_TM, nf),
        in_specs=[
            pl.BlockSpec((FFN_TM, d), lambda i, f: (i, 0)),
            pl.BlockSpec((d, FFN_TF), lambda i, f: (0, f)),
            pl.BlockSpec((d, FFN_TF), lambda i, f: (0, nf + f)),
            pl.BlockSpec((FFN_TF, d), lambda i, f: (f, 0)),
            pl.BlockSpec((FFN_TM, d), lambda i, f: (i, 0)),
            pl.BlockSpec((1, d), lambda i, f: (0, 0)),
        ],
        out_specs=pl.BlockSpec((FFN_TM, d), lambda i, f: (i, 0)),
        out_shape=jax.ShapeDtypeStruct((t, d), _F32),
        scratch_shapes=[pltpu.VMEM((FFN_TM, d), _F32)],
        compiler_params=_params(("arbitrary", "arbitrary")),
        name="ffn",
    )(h, w_gu_bf16, w_gu_bf16, w_down_bf16, x2, g_post)


def _rotary_lane_tables(positions):
    half = ROT_DIM // 2
    inv_freq = ROPE_THETA ** (-jnp.arange(0, ROT_DIM, 2, dtype=_F32) / ROT_DIM)
    ang = positions.astype(_F32).reshape(-1, 1) * inv_freq
    cos, sin = jnp.cos(ang), jnp.sin(ang)
    t = ang.shape[0]
    pad = HEAD_DIM - ROT_DIM
    ra = jnp.concatenate([cos, cos, jnp.ones((t, pad), _F32)], axis=1)
    rb = jnp.concatenate([-sin, jnp.zeros((t, half + pad), _F32)], axis=1)
    rc = jnp.concatenate([jnp.zeros((t, half), _F32), sin, jnp.zeros((t, pad), _F32)], axis=1)
    tile = lambda a: jnp.tile(a, (1, HEADS_PER_TILE))
    return tile(ra), tile(rb), tile(rc)


def kernel(x, positions, norm_gains, swa_w_in, swa_sinks, swa_w_out, fox_w_in, fox_b_f,
           fox_w_out, ffn_w_gate_up, ffn_w_down):
    batch, seq_len, d = x.shape
    depth = norm_gains.shape[0]
    hd = N_HEADS * HEAD_DIM
    assert d == D_MODEL and seq_len % PROJ_TM == 0 and seq_len % FOX_T == 0
    assert WINDOW <= BLOCK

    x2 = x.reshape(batch * seq_len, d)
    ra, rb, rc = _rotary_lane_tables(positions)
    gains = norm_gains.reshape(depth, 4, 1, d)

    for layer in range(depth):
        j = layer // 2
        g = gains[layer]
        if layer % 2 == 0:
            qkv = _proj_swa(x2, g[0], swa_w_in[j].astype(_BF16), ra, rb, rc)
            attn = _swa_attention(qkv, swa_sinks[j].astype(_F32), batch, seq_len)
            w_out = swa_w_out[j]
        else:
            w_in = fox_w_in[j]
            wf_pad = jnp.pad(w_in[:, 3 * hd:].astype(_F32), ((0, 0), (0, LANES - N_HEADS)))
            wf_hi = wf_pad.astype(_BF16)
            wf_lo = (wf_pad - wf_hi.astype(_F32)).astype(_BF16)
            bf_pad = jnp.pad(fox_b_f[j].astype(_F32), (0, LANES - N_HEADS)).reshape(1, LANES)
            qkv, ka = _proj_fox(x2, g[0], w_in.astype(_BF16),
                                jnp.concatenate([wf_hi, wf_lo], axis=1), bf_pad, seq_len)
            attn = _fox_attention(qkv, ka, batch, seq_len)
            w_out = fox_w_out[j]
        x2, h = _out_proj(attn, w_out.astype(_BF16), x2, g[1], g[2])
        x2 = _ffn(h, ffn_w_gate_up[layer].astype(_BF16), ffn_w_down[layer].astype(_BF16),
                  x2, g[3])
    return x2.reshape(batch, seq_len, d)
```

```python
import functools

import jax
import jax.numpy as jnp
from jax import lax
from jax.experimental import pallas as pl
from jax.experimental.pallas import tpu as pltpu

D_MODEL = 2048
HEAD_DIM = 64
N_HEADS = D_MODEL // HEAD_DIM
SWA_KV_HEADS = 8
SWA_GROUP = N_HEADS // SWA_KV_HEADS
WINDOW = 128
BLOCK = 128
ROPE_THETA = 500000.0
ROT_DIM = HEAD_DIM // 4
RMS_EPS = 1e-6

LANES = 128
HEADS_PER_TILE = LANES // HEAD_DIM
VMEM_LIMIT_BYTES = 56 * 1024 * 1024
MASK_VALUE = -1e30

PROJ_TM = 1024
PROJ_TN = 512
OUT_TM = 512
FFN_TM = 512
FFN_TF = 512
FOX_T = 512
SWA_TILES = 2
FOX_PAIRS = 2
BIAS_TERMS = 3
LOG2E = 1.4426950408889634

_F32 = jnp.float32
_BF16 = jnp.bfloat16
_NT = (((1,), (1,)), ((), ()))


def _params(semantics):
    return pltpu.CompilerParams(dimension_semantics=semantics,
                                vmem_limit_bytes=VMEM_LIMIT_BYTES)


def _rms_scale(v):
    return lax.rsqrt(jnp.mean(v * v, axis=-1, keepdims=True) + RMS_EPS)


def _proj_swa_kernel(x_ref, g_ref, w_ref, ra_ref, rb_ref, rc_ref, o_ref, h_ref,
                     *, n_q_blocks, n_rope_blocks):
    j = pl.program_id(1)

    @pl.when(j == 0)
    def _():
        x = x_ref[...]
        h_ref[...] = (x * _rms_scale(x) * g_ref[...]).astype(_BF16)

    res = jnp.dot(h_ref[...], w_ref[...], preferred_element_type=_F32)

    rope = (j < n_rope_blocks).astype(_F32)
    scale = jnp.where(j < n_q_blocks, HEAD_DIM ** -0.5, 1.0).astype(_F32)
    ra = (ra_ref[...] * rope + (1.0 - rope)) * scale
    rb = rb_ref[...] * (rope * scale)
    rc = rc_ref[...] * (rope * scale)
    half = ROT_DIM // 2
    for c in range(res.shape[1] // LANES):
        r = res[:, c * LANES:(c + 1) * LANES]
        rot = r * ra + pltpu.roll(r, LANES - half, 1) * rb + pltpu.roll(r, half, 1) * rc
        o_ref[:, c * LANES:(c + 1) * LANES] = rot.astype(_BF16)


def _proj_swa(x2, gain, w_all, idx, ra, rb, rc):
    t, d = x2.shape
    n = w_all.shape[2]
    qd = N_HEADS * HEAD_DIM
    kd = SWA_KV_HEADS * HEAD_DIM
    kern = functools.partial(_proj_swa_kernel, n_q_blocks=qd // PROJ_TN,
                             n_rope_blocks=(qd + kd) // PROJ_TN)
    return pl.pallas_call(
        kern,
        grid=(t // PROJ_TM, n // PROJ_TN),
        in_specs=[
            pl.BlockSpec((PROJ_TM, d), lambda i, j: (i, 0)),
            pl.BlockSpec((1, d), lambda i, j: (0, 0)),
            pl.BlockSpec((None, d, PROJ_TN), lambda i, j: (idx, 0, j)),
            pl.BlockSpec((PROJ_TM, LANES), lambda i, j: (i, 0)),
            pl.BlockSpec((PROJ_TM, LANES), lambda i, j: (i, 0)),
            pl.BlockSpec((PROJ_TM, LANES), lambda i, j: (i, 0)),
        ],
        out_specs=pl.BlockSpec((PROJ_TM, PROJ_TN), lambda i, j: (i, j)),
        out_shape=jax.ShapeDtypeStruct((t, n), _BF16),
        scratch_shapes=[pltpu.VMEM((PROJ_TM, d), _BF16)],
        compiler_params=_params(("arbitrary", "arbitrary")),
        name="proj_swa",
    )(x2, gain, w_all, ra, rb, rc)


def _pack_terms(v, lane):
    t0 = v.astype(_BF16).astype(_F32)
    r1 = v - t0
    t1 = r1.astype(_BF16).astype(_F32)
    t2 = (r1 - t1).astype(_BF16).astype(_F32)
    packed = jnp.where(lane < N_HEADS, t0,
                       jnp.where(lane < 2 * N_HEADS, pltpu.roll(t1, N_HEADS, 1),
                                 jnp.where(lane < 3 * N_HEADS, pltpu.roll(t2, 2 * N_HEADS, 1), 0.0)))
    return packed.astype(_BF16)


def _proj_fox_kernel(x_ref, g_ref, w_ref, wf_ref, bf_ref, tri_ref, sel_ref, o_ref, ka_ref,
                     h_ref, carry_ref, *, n_q_blocks, blocks_per_seq):
    i = pl.program_id(0)
    j = pl.program_id(1)

    @pl.when(j == 0)
    def _():
        x = x_ref[...]
        hf = x * _rms_scale(x) * g_ref[...]
        h_hi = hf.astype(_BF16)
        h_ref[...] = h_hi
        h_lo = (hf - h_hi.astype(_F32)).astype(_BF16)
        both = jnp.dot(h_hi, wf_ref[...], preferred_element_type=_F32)
        logit = (both[:, :LANES] + both[:, LANES:]
                 + jnp.dot(h_lo, wf_ref[:, :LANES], preferred_element_type=_F32))
        z = logit + bf_ref[...]
        log_f = jnp.minimum(z, 0.0) - jnp.log1p(jnp.exp(-jnp.abs(z)))
        lane = lax.broadcasted_iota(jnp.int32, log_f.shape, 1)
        part = jnp.dot(tri_ref[...], _pack_terms(log_f, lane), preferred_element_type=_F32)
        csum = (part + pltpu.roll(part, LANES - N_HEADS, 1)
                + pltpu.roll(part, LANES - 2 * N_HEADS, 1))

        @pl.when(i % blocks_per_seq == 0)
        def _():
            carry_ref[...] = jnp.zeros_like(carry_ref)

        c = csum + carry_ref[...]
        carry_ref[...] = c[PROJ_TM - 1:PROJ_TM, :]
        ka = jnp.dot(_pack_terms(c * -LOG2E, lane), sel_ref[...], preferred_element_type=_F32)
        ka_ref[...] = ka.astype(_BF16)

    res = jnp.dot(h_ref[...], w_ref[...], preferred_element_type=_F32)
    scale = jnp.where(j < n_q_blocks, HEAD_DIM ** -0.5 * LOG2E, 1.0).astype(_F32)
    o_ref[...] = (res * scale).astype(_BF16)


def _bias_lane_selector():
    n_pairs = N_HEADS // HEADS_PER_TILE
    row = jnp.arange(LANES)[:, None]
    s, h = row // N_HEADS, row % N_HEADS
    col = jnp.arange(n_pairs * LANES)[None, :]
    target = (h // HEADS_PER_TILE) * LANES + BIAS_TERMS * (h % HEADS_PER_TILE) + s
    return ((col == target) & (s < BIAS_TERMS)).astype(_BF16)


def _proj_fox(x2, gain, w_all, idx, wf_hi_lo, bf_pad, seq_len):
    t, d = x2.shape
    n = 3 * N_HEADS * HEAD_DIM
    ka_cols = (N_HEADS // HEADS_PER_TILE) * LANES
    tri = (lax.broadcasted_iota(jnp.int32, (PROJ_TM, PROJ_TM), 1)
           <= lax.broadcasted_iota(jnp.int32, (PROJ_TM, PROJ_TM), 0)).astype(_BF16)
    kern = functools.partial(_proj_fox_kernel,
                             n_q_blocks=(N_HEADS * HEAD_DIM) // PROJ_TN,
                             blocks_per_seq=seq_len // PROJ_TM)
    return pl.pallas_call(
        kern,
        grid=(t // PROJ_TM, n // PROJ_TN),
        in_specs=[
            pl.BlockSpec((PROJ_TM, d), lambda i, j: (i, 0)),
            pl.BlockSpec((1, d), lambda i, j: (0, 0)),
            pl.BlockSpec((None, d, PROJ_TN), lambda i, j: (idx, 0, j)),
            pl.BlockSpec((d, 2 * LANES), lambda i, j: (0, 0)),
            pl.BlockSpec((1, LANES), lambda i, j: (0, 0)),
            pl.BlockSpec((PROJ_TM, PROJ_TM), lambda i, j: (0, 0)),
            pl.BlockSpec((LANES, ka_cols), lambda i, j: (0, 0)),
        ],
        out_specs=[
            pl.BlockSpec((PROJ_TM, PROJ_TN), lambda i, j: (i, j)),
            pl.BlockSpec((PROJ_TM, ka_cols), lambda i, j: (i, 0)),
        ],
        out_shape=[jax.ShapeDtypeStruct((t, n), _BF16),
                   jax.ShapeDtypeStruct((t, ka_cols), _BF16)],
        scratch_shapes=[pltpu.VMEM((PROJ_TM, d), _BF16),
                        pltpu.VMEM((1, LANES), _F32)],
        compiler_params=_params(("arbitrary", "arbitrary")),
        name="proj_fox",
    )(x2, gain, w_all, wf_hi_lo, bf_pad, tri, _bias_lane_selector())


def _swa_kernel(sink_ref, q_ref, kp_ref, kc_ref, vp_ref, vc_ref, o_ref):
    g = pl.program_id(1)
    n = pl.program_id(2)
    lane = lax.broadcasted_iota(jnp.int32, (1, LANES), 1)
    low = lane < HEAD_DIM

    def both_halves(prev_ref, cur_ref, tile):
        cols = slice(tile * LANES, (tile + 1) * LANES)
        band = jnp.concatenate([prev_ref[:, cols], cur_ref[:, cols]], axis=0).astype(_F32)
        swapped = pltpu.roll(band, HEAD_DIM, 1)
        return (jnp.where(low, band, swapped).astype(_BF16),
                jnp.where(low, swapped, band).astype(_BF16))

    q_pos = lax.broadcasted_iota(jnp.int32, (BLOCK, 2 * BLOCK), 0)
    s_pos = lax.broadcasted_iota(jnp.int32, (BLOCK, 2 * BLOCK), 1)
    rel = BLOCK + q_pos - s_pos
    valid = (rel >= 0) & (rel < WINDOW) & ((s_pos >= BLOCK) | (n > 0))

    zero = jnp.zeros((), _BF16)
    chunks_per_kv = SWA_GROUP // HEADS_PER_TILE
    dups = [(both_halves(kp_ref, kc_ref, tile), both_halves(vp_ref, vc_ref, tile))
            for tile in range(SWA_TILES)]
    for tile, kv in [(tile, kv) for tile in range(SWA_TILES) for kv in range(HEADS_PER_TILE)]:
        k_dup, v_dup = dups[tile]
        kv_head = tile * HEADS_PER_TILE + kv
        chunks = [q_ref[:, (chunks_per_kv * kv_head + c) * LANES:(chunks_per_kv * kv_head + c + 1) * LANES]
                  for c in range(chunks_per_kv)]
        q_stack = jnp.concatenate(
            [jnp.where(low if hh == 0 else ~low, qc, zero)
             for qc in chunks for hh in range(HEADS_PER_TILE)], axis=0)
        s_all = lax.dot_general(q_stack, k_dup[kv], _NT, preferred_element_type=_F32)
        probs, inv = [], []
        for h in range(SWA_GROUP):
            s = jnp.where(valid, s_all[h * BLOCK:(h + 1) * BLOCK], MASK_VALUE)
            sink = sink_ref[(g * SWA_TILES * HEADS_PER_TILE + kv_head) * SWA_GROUP + h]
            m = jnp.maximum(jnp.max(s, axis=-1, keepdims=True), sink)
            p = jnp.exp(s - m)
            denom = jnp.sum(p, axis=-1, keepdims=True) + jnp.exp(sink - m)
            probs.append(p.astype(_BF16))
            inv.append(1.0 / denom)
        o_all = jnp.dot(jnp.concatenate(probs, axis=0), v_dup[kv],
                        preferred_element_type=_F32)
        for c in range(chunks_per_kv):
            h0, h1 = HEADS_PER_TILE * c, HEADS_PER_TILE * c + 1
            o0 = o_all[h0 * BLOCK:(h0 + 1) * BLOCK] * inv[h0]
            o1 = o_all[h1 * BLOCK:(h1 + 1) * BLOCK] * inv[h1]
            col = (chunks_per_kv * kv_head + c) * LANES
            o_ref[:, col:col + LANES] = jnp.where(low, o0, o1).astype(_BF16)


def _swa_attention(qkv, sinks, batch, seq_len):
    t = qkv.shape[0]
    nblk = seq_len // BLOCK
    qd = N_HEADS * HEAD_DIM
    kd = SWA_KV_HEADS * HEAD_DIM
    kv_tile = SWA_TILES * LANES
    q_tile = kv_tile * SWA_GROUP
    k_col0 = qd // kv_tile
    v_col0 = (qd + kd) // kv_tile

    def cur(col0):
        return lambda b, g, n: (b * nblk + n, col0 + g)

    def prev(col0):
        return lambda b, g, n: (b * nblk + jnp.maximum(n - 1, 0), col0 + g)

    return pl.pallas_call(
        _swa_kernel,
        grid=(batch, kd // kv_tile, nblk),
        in_specs=[
            pl.BlockSpec(memory_space=pltpu.SMEM),
            pl.BlockSpec((BLOCK, q_tile), lambda b, g, n: (b * nblk + n, g)),
            pl.BlockSpec((BLOCK, kv_tile), prev(k_col0)),
            pl.BlockSpec((BLOCK, kv_tile), cur(k_col0)),
            pl.BlockSpec((BLOCK, kv_tile), prev(v_col0)),
            pl.BlockSpec((BLOCK, kv_tile), cur(v_col0)),
        ],
        out_specs=pl.BlockSpec((BLOCK, q_tile), lambda b, g, n: (b * nblk + n, g)),
        out_shape=jax.ShapeDtypeStruct((t, qd), _BF16),
        compiler_params=_params(("arbitrary", "arbitrary", "arbitrary")),
        name="swa_attn",
    )(sinks, qkv, qkv, qkv, qkv, qkv)


def _fox_kernel(q_ref, k_ref, ka_ref, v_ref, o_ref, qa_ref, s_ref, m_ref, acc_ref):
    i = pl.program_id(2)
    t = FOX_T
    lane = lax.broadcasted_iota(jnp.int32, (1, LANES), 1)
    low = lane < HEAD_DIM
    zero = jnp.zeros((), _BF16)
    heads = [(pr, hd) for pr in range(FOX_PAIRS) for hd in range(HEADS_PER_TILE)]

    def lanes_of(pr):
        return slice(pr * LANES, (pr + 1) * LANES)

    for h, (pr, hd) in enumerate(heads):
        own = low if hd == 0 else ~low
        bias_lanes = (lane >= BIAS_TERMS * hd) & (lane < BIAS_TERMS * (hd + 1))
        pick = jnp.broadcast_to(jnp.where(bias_lanes, 1.0, 0.0).astype(_BF16), (t, LANES))
        qa_ref[h] = jnp.concatenate([jnp.where(own, q_ref[:, lanes_of(pr)], zero), pick], axis=1)
        m_ref[h] = jnp.full((t, LANES), MASK_VALUE, _F32)
        acc_ref[h] = jnp.zeros((t, 2 * LANES), _F32)

    def scores(kb, slot):
        ks = pl.multiple_of(kb * t, t)
        for pr in range(FOX_PAIRS):
            kt = jnp.concatenate([k_ref[pl.ds(ks, t), lanes_of(pr)],
                                  ka_ref[pl.ds(ks, t), lanes_of(pr)]], axis=1)
            for hd in range(HEADS_PER_TILE):
                h = HEADS_PER_TILE * pr + hd
                s_ref[slot, h] = lax.dot_general(qa_ref[h], kt, _NT, preferred_element_type=_F32)

    def absorb(kb, slot, masked):
        ks = pl.multiple_of(kb * t, t)
        ones = jnp.ones((t, LANES), _BF16)
        for h, (pr, hd) in enumerate(heads):
            vt = jnp.concatenate([v_ref[pl.ds(ks, t), lanes_of(pr)], ones], axis=1)
            s = s_ref[slot, h]
            if masked:
                causal = (lax.broadcasted_iota(jnp.int32, (t, t), 1)
                          <= lax.broadcasted_iota(jnp.int32, (t, t), 0))
                s = jnp.where(causal, s, MASK_VALUE)
            m = m_ref[h]
            m_new = jnp.maximum(m, jnp.max(s, axis=-1, keepdims=True))
            p = jnp.exp2(s - jnp.concatenate([m_new] * (t // LANES), axis=1)).astype(_BF16)
            alpha = jnp.exp2(m - m_new)
            acc_ref[h] = (jnp.concatenate([alpha, alpha], axis=1) * acc_ref[h]
                          + jnp.dot(p, vt, preferred_element_type=_F32))
            m_ref[h] = m_new

    scores(0, 0)

    def two_blocks(j, carry):
        kb = 2 * j
        scores(kb + 1, 1)
        absorb(kb, 0, False)
        scores(kb + 2, 0)
        absorb(kb + 1, 1, False)
        return carry

    lax.fori_loop(0, lax.shift_right_logical(i, 1), two_blocks, 0)

    @pl.when((i & 1) == 1)
    def _():
        scores(i, 1)
        absorb(i - 1, 0, False)
        absorb(i, 1, True)

    @pl.when((i & 1) == 0)
    def _():
        absorb(i, 0, True)

    for pr in range(FOX_PAIRS):
        outs = [acc_ref[HEADS_PER_TILE * pr + hd][:, :LANES] / acc_ref[HEADS_PER_TILE * pr + hd][:, LANES:]
                for hd in range(HEADS_PER_TILE)]
        o_ref[:, lanes_of(pr)] = jnp.where(low, outs[0], outs[1]).astype(_BF16)


def _fox_attention(qkv, ka, batch, seq_len):
    t = qkv.shape[0]
    hd = N_HEADS * HEAD_DIM
    n_steps = N_HEADS // (HEADS_PER_TILE * FOX_PAIRS)
    width = FOX_PAIRS * LANES
    heads = FOX_PAIRS * HEADS_PER_TILE
    nq = seq_len // FOX_T
    return pl.pallas_call(
        _fox_kernel,
        grid=(batch, n_steps, nq),
        in_specs=[
            pl.BlockSpec((FOX_T, width), lambda b, p, i: (b * nq + i, p)),
            pl.BlockSpec((seq_len, width), lambda b, p, i: (b, n_steps + p)),
            pl.BlockSpec((seq_len, width), lambda b, p, i: (b, p)),
            pl.BlockSpec((seq_len, width), lambda b, p, i: (b, 2 * n_steps + p)),
        ],
        out_specs=pl.BlockSpec((FOX_T, width), lambda b, p, i: (b * nq + i, p)),
        out_shape=jax.ShapeDtypeStruct((t, hd), _BF16),
        scratch_shapes=[
            pltpu.VMEM((heads, FOX_T, 2 * LANES), _BF16),
            pltpu.VMEM((2, heads, FOX_T, FOX_T), _F32),
            pltpu.VMEM((heads, FOX_T, LANES), _F32),
            pltpu.VMEM((heads, FOX_T, 2 * LANES), _F32),
        ],
        compiler_params=_params(("arbitrary", "arbitrary", "arbitrary")),
        name="fox_attn",
    )(qkv, qkv, ka, qkv)


def _out_kernel(a_ref, w_ref, x_ref, g1_ref, g2_ref, xo_ref, h_ref):
    y = jnp.dot(a_ref[...], w_ref[...], preferred_element_type=_F32)
    xn = x_ref[...] + y * _rms_scale(y) * g1_ref[...]
    xo_ref[...] = xn
    h_ref[...] = (xn * _rms_scale(xn) * g2_ref[...]).astype(_BF16)


def _out_proj(attn, w_all, idx, x2, g_post, g_pre_ffn):
    t, d = x2.shape
    k = attn.shape[1]
    row = lambda i: (i, 0)
    fixed = lambda i: (0, 0)
    return pl.pallas_call(
        _out_kernel,
        grid=(t // OUT_TM,),
        in_specs=[
            pl.BlockSpec((OUT_TM, k), row),
            pl.BlockSpec((None, k, d), lambda i: (idx, 0, 0)),
            pl.BlockSpec((OUT_TM, d), row),
            pl.BlockSpec((1, d), fixed),
            pl.BlockSpec((1, d), fixed),
        ],
        out_specs=[pl.BlockSpec((OUT_TM, d), row), pl.BlockSpec((OUT_TM, d), row)],
        out_shape=[jax.ShapeDtypeStruct((t, d), _F32), jax.ShapeDtypeStruct((t, d), _BF16)],
        compiler_params=_params(("arbitrary",)),
        name="out_proj",
    )(attn, w_all, x2, g_post, g_pre_ffn)


def _ffn_kernel(h_ref, wg_ref, wu_ref, wd_ref, x_ref, g_ref, o_ref, acc_ref):
    f = pl.program_id(1)

    @pl.when(f == 0)
    def _():
        acc_ref[...] = jnp.zeros_like(acc_ref)

    h = h_ref[...]
    gate = jnp.dot(h, wg_ref[...], preferred_element_type=_F32)
    up = jnp.dot(h, wu_ref[...], preferred_element_type=_F32)
    act = (gate * (1.0 / (1.0 + jnp.exp(-gate))) * up).astype(_BF16)
    acc_ref[...] += jnp.dot(act, wd_ref[...], preferred_element_type=_F32)

    @pl.when(f == pl.num_programs(1) - 1)
    def _():
        y = acc_ref[...]
        o_ref[...] = x_ref[...] + y * _rms_scale(y) * g_ref[...]


def _ffn(h, w_gu_all, w_down_all, idx, x2, g_post):
    t, d = x2.shape
    d_ff = w_down_all.shape[1]
    nf = d_ff // FFN_TF
    return pl.pallas_call(
        _ffn_kernel,
        grid=(t // FFN_TM, nf),
        in_specs=[
            pl.BlockSpec((FFN_TM, d), lambda i, f: (i, 0)),
            pl.BlockSpec((None, d, FFN_TF), lambda i, f: (idx, 0, f)),
            pl.BlockSpec((None, d, FFN_TF), lambda i, f: (idx, 0, nf + f)),
            pl.BlockSpec((None, FFN_TF, d), lambda i, f: (idx, f, 0)),
            pl.BlockSpec((FFN_TM, d), lambda i, f: (i, 0)),
            pl.BlockSpec((1, d), lambda i, f: (0, 0)),
        ],
        out_specs=pl.BlockSpec((FFN_TM, d), lambda i, f: (i, 0)),
        out_shape=jax.ShapeDtypeStruct((t, d), _F32),
        scratch_shapes=[pltpu.VMEM((FFN_TM, d), _F32)],
        compiler_params=_params(("arbitrary", "arbitrary")),
        name="ffn",
    )(h, w_gu_all, w_gu_all, w_down_all, x2, g_post)


def _rotary_lane_tables(positions):
    half = ROT_DIM // 2
    inv_freq = ROPE_THETA ** (-jnp.arange(0, ROT_DIM, 2, dtype=_F32) / ROT_DIM)
    ang = positions.astype(_F32).reshape(-1, 1) * inv_freq
    cos, sin = jnp.cos(ang), jnp.sin(ang)
    t = ang.shape[0]
    pad = HEAD_DIM - ROT_DIM
    ra = jnp.concatenate([cos, cos, jnp.ones((t, pad), _F32)], axis=1)
    rb = jnp.concatenate([-sin, jnp.zeros((t, half + pad), _F32)], axis=1)
    rc = jnp.concatenate([jnp.zeros((t, half), _F32), sin, jnp.zeros((t, pad), _F32)], axis=1)
    tile = lambda a: jnp.tile(a, (1, HEADS_PER_TILE))
    return tile(ra), tile(rb), tile(rc)


def kernel(x, positions, norm_gains, swa_w_in, swa_sinks, swa_w_out, fox_w_in, fox_b_f,
           fox_w_out, ffn_w_gate_up, ffn_w_down):
    batch, seq_len, d = x.shape
    depth = norm_gains.shape[0]
    hd = N_HEADS * HEAD_DIM
    assert d == D_MODEL and seq_len % PROJ_TM == 0 and seq_len % FOX_T == 0
    assert WINDOW <= BLOCK

    x2 = x.reshape(batch * seq_len, d)
    ra, rb, rc = _rotary_lane_tables(positions)
    gains = norm_gains.reshape(depth, 4, 1, d)
    swa_w_in_b, swa_w_out_b = swa_w_in.astype(_BF16), swa_w_out.astype(_BF16)
    fox_w_in_b, fox_w_out_b = fox_w_in.astype(_BF16), fox_w_out.astype(_BF16)
    ffn_w_gu_b, ffn_w_down_b = ffn_w_gate_up.astype(_BF16), ffn_w_down.astype(_BF16)

    for layer in range(depth):
        j = layer // 2
        g = gains[layer]
        if layer % 2 == 0:
            qkv = _proj_swa(x2, g[0], swa_w_in_b, j, ra, rb, rc)
            attn = _swa_attention(qkv, swa_sinks[j].astype(_F32), batch, seq_len)
            w_out = swa_w_out_b
        else:
            wf_pad = jnp.pad(fox_w_in[j][:, 3 * hd:].astype(_F32), ((0, 0), (0, LANES - N_HEADS)))
            wf_hi = wf_pad.astype(_BF16)
            wf_lo = (wf_pad - wf_hi.astype(_F32)).astype(_BF16)
            bf_pad = jnp.pad(fox_b_f[j].astype(_F32), (0, LANES - N_HEADS)).reshape(1, LANES)
            qkv, ka = _proj_fox(x2, g[0], fox_w_in_b, j,
                                jnp.concatenate([wf_hi, wf_lo], axis=1), bf_pad, seq_len)
            attn = _fox_attention(qkv, ka, batch, seq_len)
            w_out = fox_w_out_b
        x2, h = _out_proj(attn, w_out, j, x2, g[1], g[2])
        x2 = _ffn(h, ffn_w_gu_b, ffn_w_down_b, layer, x2, g[3])
    return x2.reshape(batch, seq_len, d)
```

```python
import functools

import jax
import jax.numpy as jnp
from jax import lax
from jax.experimental import pallas as pl
from jax.experimental.pallas import tpu as pltpu

D_MODEL = 2048
HEAD_DIM = 64
N_HEADS = D_MODEL // HEAD_DIM
SWA_KV_HEADS = 8
SWA_GROUP = N_HEADS // SWA_KV_HEADS
WINDOW = 128
BLOCK = 128
ROPE_THETA = 500000.0
ROT_DIM = HEAD_DIM // 4
RMS_EPS = 1e-6

LANES = 128
HEADS_PER_TILE = LANES // HEAD_DIM
VMEM_LIMIT_BYTES = 56 * 1024 * 1024
MASK_VALUE = -1e30

PROJ_TM = 1024
PROJ_TN = 512
OUT_TM = 512
FFN_TM = 512
FFN_TF = 512
FOX_T = 512
SWA_TILES = 2
FOX_PAIRS = 2
BIAS_TERMS = 3
LOG2E = 1.4426950408889634

_F32 = jnp.float32
_BF16 = jnp.bfloat16
_NT = (((1,), (1,)), ((), ()))


def _params(semantics):
    return pltpu.CompilerParams(dimension_semantics=semantics,
                                vmem_limit_bytes=VMEM_LIMIT_BYTES)


def _rms_scale(v):
    return lax.rsqrt(jnp.mean(v * v, axis=-1, keepdims=True) + RMS_EPS)


def _proj_swa_kernel(x_ref, g_ref, w_ref, ra_ref, rb_ref, rc_ref, o_ref, h_ref,
                     *, n_q_blocks, n_rope_blocks):
    j = pl.program_id(1)

    @pl.when(j == 0)
    def _():
        x = x_ref[...]
        h_ref[...] = (x * _rms_scale(x) * g_ref[...]).astype(_BF16)

    res = jnp.dot(h_ref[...], w_ref[...], preferred_element_type=_F32)

    rope = (j < n_rope_blocks).astype(_F32)
    scale = jnp.where(j < n_q_blocks, HEAD_DIM ** -0.5, 1.0).astype(_F32)
    ra = (ra_ref[...] * rope + (1.0 - rope)) * scale
    rb = rb_ref[...] * (rope * scale)
    rc = rc_ref[...] * (rope * scale)
    half = ROT_DIM // 2
    for c in range(res.shape[1] // LANES):
        r = res[:, c * LANES:(c + 1) * LANES]
        rot = r * ra + pltpu.roll(r, LANES - half, 1) * rb + pltpu.roll(r, half, 1) * rc
        o_ref[:, c * LANES:(c + 1) * LANES] = rot.astype(_BF16)


def _proj_swa(x2, gain, w_all, idx, ra, rb, rc):
    t, d = x2.shape
    n = w_all.shape[2]
    qd = N_HEADS * HEAD_DIM
    kd = SWA_KV_HEADS * HEAD_DIM
    kern = functools.partial(_proj_swa_kernel, n_q_blocks=qd // PROJ_TN,
                             n_rope_blocks=(qd + kd) // PROJ_TN)
    return pl.pallas_call(
        kern,
        grid=(t // PROJ_TM, n // PROJ_TN),
        in_specs=[
            pl.BlockSpec((PROJ_TM, d), lambda i, j: (i, 0)),
            pl.BlockSpec((1, d), lambda i, j: (0, 0)),
            pl.BlockSpec((None, d, PROJ_TN), lambda i, j: (idx, 0, j)),
            pl.BlockSpec((PROJ_TM, LANES), lambda i, j: (i, 0)),
            pl.BlockSpec((PROJ_TM, LANES), lambda i, j: (i, 0)),
            pl.BlockSpec((PROJ_TM, LANES), lambda i, j: (i, 0)),
        ],
        out_specs=pl.BlockSpec((PROJ_TM, PROJ_TN), lambda i, j: (i, j)),
        out_shape=jax.ShapeDtypeStruct((t, n), _BF16),
        scratch_shapes=[pltpu.VMEM((PROJ_TM, d), _BF16)],
        compiler_params=_params(("arbitrary", "arbitrary")),
        name="proj_swa",
    )(x2, gain, w_all, ra, rb, rc)


def _pack_terms(v, lane):
    t0 = v.astype(_BF16).astype(_F32)
    r1 = v - t0
    t1 = r1.astype(_BF16).astype(_F32)
    t2 = (r1 - t1).astype(_BF16).astype(_F32)
    packed = jnp.where(lane < N_HEADS, t0,
                       jnp.where(lane < 2 * N_HEADS, pltpu.roll(t1, N_HEADS, 1),
                                 jnp.where(lane < 3 * N_HEADS, pltpu.roll(t2, 2 * N_HEADS, 1), 0.0)))
    return packed.astype(_BF16)


def _proj_fox_kernel(x_ref, g_ref, w_ref, wf_ref, bf_ref, tri_ref, sel_ref, o_ref, ka_ref,
                     h_ref, carry_ref, *, n_q_blocks, blocks_per_seq):
    i = pl.program_id(0)
    j = pl.program_id(1)

    @pl.when(j == 0)
    def _():
        x = x_ref[...]
        hf = x * _rms_scale(x) * g_ref[...]
        h_hi = hf.astype(_BF16)
        h_ref[...] = h_hi
        h_lo = (hf - h_hi.astype(_F32)).astype(_BF16)
        both = lax.dot_general(h_hi, wf_ref[...], _NT, preferred_element_type=_F32)
        logit = (both[:, :LANES] + both[:, LANES:]
                 + lax.dot_general(h_lo, wf_ref[:LANES, :], _NT,
                                   preferred_element_type=_F32))
        z = logit + bf_ref[...]
        log_f = jnp.minimum(z, 0.0) - jnp.log1p(jnp.exp(-jnp.abs(z)))
        lane = lax.broadcasted_iota(jnp.int32, log_f.shape, 1)
        part = jnp.dot(tri_ref[...], _pack_terms(log_f, lane), preferred_element_type=_F32)
        csum = (part + pltpu.roll(part, LANES - N_HEADS, 1)
                + pltpu.roll(part, LANES - 2 * N_HEADS, 1))

        @pl.when(i % blocks_per_seq == 0)
        def _():
            carry_ref[...] = jnp.zeros_like(carry_ref)

        c = csum + carry_ref[...]
        carry_ref[...] = c[PROJ_TM - 1:PROJ_TM, :]
        ka = jnp.dot(_pack_terms(c * -LOG2E, lane), sel_ref[...], preferred_element_type=_F32)
        ka_ref[...] = ka.astype(_BF16)

    res = lax.dot_general(h_ref[...], w_ref[...], _NT, preferred_element_type=_F32)
    scale = jnp.where(j < n_q_blocks, HEAD_DIM ** -0.5 * LOG2E, 1.0).astype(_F32)
    o_ref[...] = (res * scale).astype(_BF16)


def _bias_lane_selector():
    n_pairs = N_HEADS // HEADS_PER_TILE
    row = jnp.arange(LANES)[:, None]
    s, h = row // N_HEADS, row % N_HEADS
    col = jnp.arange(n_pairs * LANES)[None, :]
    target = (h // HEADS_PER_TILE) * LANES + BIAS_TERMS * (h % HEADS_PER_TILE) + s
    return ((col == target) & (s < BIAS_TERMS)).astype(_BF16)


def _proj_fox(x2, gain, w_all, idx, wf_hi_lo, bf_pad, seq_len):
    t, d = x2.shape
    n = 3 * N_HEADS * HEAD_DIM
    ka_cols = (N_HEADS // HEADS_PER_TILE) * LANES
    tri = (lax.broadcasted_iota(jnp.int32, (PROJ_TM, PROJ_TM), 1)
           <= lax.broadcasted_iota(jnp.int32, (PROJ_TM, PROJ_TM), 0)).astype(_BF16)
    kern = functools.partial(_proj_fox_kernel,
                             n_q_blocks=(N_HEADS * HEAD_DIM) // PROJ_TN,
                             blocks_per_seq=seq_len // PROJ_TM)
    return pl.pallas_call(
        kern,
        grid=(t // PROJ_TM, n // PROJ_TN),
        in_specs=[
            pl.BlockSpec((PROJ_TM, d), lambda i, j: (i, 0)),
            pl.BlockSpec((1, d), lambda i, j: (0, 0)),
            pl.BlockSpec((None, PROJ_TN, d), lambda i, j: (idx, j, 0)),
            pl.BlockSpec((2 * LANES, d), lambda i, j: (0, 0)),
            pl.BlockSpec((1, LANES), lambda i, j: (0, 0)),
            pl.BlockSpec((PROJ_TM, PROJ_TM), lambda i, j: (0, 0)),
            pl.BlockSpec((LANES, ka_cols), lambda i, j: (0, 0)),
        ],
        out_specs=[
            pl.BlockSpec((PROJ_TM, PROJ_TN), lambda i, j: (i, j)),
            pl.BlockSpec((PROJ_TM, ka_cols), lambda i, j: (i, 0)),
        ],
        out_shape=[jax.ShapeDtypeStruct((t, n), _BF16),
                   jax.ShapeDtypeStruct((t, ka_cols), _BF16)],
        scratch_shapes=[pltpu.VMEM((PROJ_TM, d), _BF16),
                        pltpu.VMEM((1, LANES), _F32)],
        compiler_params=_params(("arbitrary", "arbitrary")),
        name="proj_fox",
    )(x2, gain, w_all, wf_hi_lo, bf_pad, tri, _bias_lane_selector())


def _swa_kernel(sink_ref, q_ref, kp_ref, kc_ref, vp_ref, vc_ref, o_ref):
    g = pl.program_id(1)
    n = pl.program_id(2)
    lane = lax.broadcasted_iota(jnp.int32, (1, LANES), 1)
    low = lane < HEAD_DIM

    def both_halves(prev_ref, cur_ref, tile):
        cols = slice(tile * LANES, (tile + 1) * LANES)
        band = jnp.concatenate([prev_ref[:, cols], cur_ref[:, cols]], axis=0).astype(_F32)
        swapped = pltpu.roll(band, HEAD_DIM, 1)
        return (jnp.where(low, band, swapped).astype(_BF16),
                jnp.where(low, swapped, band).astype(_BF16))

    q_pos = lax.broadcasted_iota(jnp.int32, (BLOCK, 2 * BLOCK), 0)
    s_pos = lax.broadcasted_iota(jnp.int32, (BLOCK, 2 * BLOCK), 1)
    rel = BLOCK + q_pos - s_pos
    valid = (rel >= 0) & (rel < WINDOW) & ((s_pos >= BLOCK) | (n > 0))

    zero = jnp.zeros((), _BF16)
    chunks_per_kv = SWA_GROUP // HEADS_PER_TILE
    dups = [(both_halves(kp_ref, kc_ref, tile), both_halves(vp_ref, vc_ref, tile))
            for tile in range(SWA_TILES)]
    for tile, kv in [(tile, kv) for tile in range(SWA_TILES) for kv in range(HEADS_PER_TILE)]:
        k_dup, v_dup = dups[tile]
        kv_head = tile * HEADS_PER_TILE + kv
        chunks = [q_ref[:, (chunks_per_kv * kv_head + c) * LANES:(chunks_per_kv * kv_head + c + 1) * LANES]
                  for c in range(chunks_per_kv)]
        q_stack = jnp.concatenate(
            [jnp.where(low if hh == 0 else ~low, qc, zero)
             for qc in chunks for hh in range(HEADS_PER_TILE)], axis=0)
        s_all = lax.dot_general(q_stack, k_dup[kv], _NT, preferred_element_type=_F32)
        probs, inv = [], []
        for h in range(SWA_GROUP):
            s = jnp.where(valid, s_all[h * BLOCK:(h + 1) * BLOCK], MASK_VALUE)
            sink = sink_ref[(g * SWA_TILES * HEADS_PER_TILE + kv_head) * SWA_GROUP + h]
            m = jnp.maximum(jnp.max(s, axis=-1, keepdims=True), sink)
            p = jnp.exp(s - m)
            denom = jnp.sum(p, axis=-1, keepdims=True) + jnp.exp(sink - m)
            probs.append(p.astype(_BF16))
            inv.append(1.0 / denom)
        o_all = jnp.dot(jnp.concatenate(probs, axis=0), v_dup[kv],
                        preferred_element_type=_F32)
        for c in range(chunks_per_kv):
            h0, h1 = HEADS_PER_TILE * c, HEADS_PER_TILE * c + 1
            o0 = o_all[h0 * BLOCK:(h0 + 1) * BLOCK] * inv[h0]
            o1 = o_all[h1 * BLOCK:(h1 + 1) * BLOCK] * inv[h1]
            col = (chunks_per_kv * kv_head + c) * LANES
            o_ref[:, col:col + LANES] = jnp.where(low, o0, o1).astype(_BF16)


def _swa_attention(qkv, sinks, batch, seq_len):
    t = qkv.shape[0]
    nblk = seq_len // BLOCK
    qd = N_HEADS * HEAD_DIM
    kd = SWA_KV_HEADS * HEAD_DIM
    kv_tile = SWA_TILES * LANES
    q_tile = kv_tile * SWA_GROUP
    k_col0 = qd // kv_tile
    v_col0 = (qd + kd) // kv_tile

    def cur(col0):
        return lambda b, g, n: (b * nblk + n, col0 + g)

    def prev(col0):
        return lambda b, g, n: (b * nblk + jnp.maximum(n - 1, 0), col0 + g)

    return pl.pallas_call(
        _swa_kernel,
        grid=(batch, kd // kv_tile, nblk),
        in_specs=[
            pl.BlockSpec(memory_space=pltpu.SMEM),
            pl.BlockSpec((BLOCK, q_tile), lambda b, g, n: (b * nblk + n, g)),
            pl.BlockSpec((BLOCK, kv_tile), prev(k_col0)),
            pl.BlockSpec((BLOCK, kv_tile), cur(k_col0)),
            pl.BlockSpec((BLOCK, kv_tile), prev(v_col0)),
            pl.BlockSpec((BLOCK, kv_tile), cur(v_col0)),
        ],
        out_specs=pl.BlockSpec((BLOCK, q_tile), lambda b, g, n: (b * nblk + n, g)),
        out_shape=jax.ShapeDtypeStruct((t, qd), _BF16),
        compiler_params=_params(("arbitrary", "arbitrary", "arbitrary")),
        name="swa_attn",
    )(sinks, qkv, qkv, qkv, qkv, qkv)


def _fox_kernel(q_ref, k_ref, ka_ref, v_ref, o_ref, qa_ref, s_ref, m_ref, acc_ref):
    i = pl.program_id(2)
    t = FOX_T
    lane = lax.broadcasted_iota(jnp.int32, (1, LANES), 1)
    low = lane < HEAD_DIM
    zero = jnp.zeros((), _BF16)
    heads = [(pr, hd) for pr in range(FOX_PAIRS) for hd in range(HEADS_PER_TILE)]

    def lanes_of(pr):
        return slice(pr * LANES, (pr + 1) * LANES)

    for h, (pr, hd) in enumerate(heads):
        own = low if hd == 0 else ~low
        bias_lanes = (lane >= BIAS_TERMS * hd) & (lane < BIAS_TERMS * (hd + 1))
        pick = jnp.broadcast_to(jnp.where(bias_lanes, 1.0, 0.0).astype(_BF16), (t, LANES))
        qa_ref[h] = jnp.concatenate([jnp.where(own, q_ref[:, lanes_of(pr)], zero), pick], axis=1)
        m_ref[h] = jnp.full((t, LANES), MASK_VALUE, _F32)
        acc_ref[h] = jnp.zeros((t, 2 * LANES), _F32)

    def scores(kb, slot):
        ks = pl.multiple_of(kb * t, t)
        for pr in range(FOX_PAIRS):
            kt = jnp.concatenate([k_ref[pl.ds(ks, t), lanes_of(pr)],
                                  ka_ref[pl.ds(ks, t), lanes_of(pr)]], axis=1)
            for hd in range(HEADS_PER_TILE):
                h = HEADS_PER_TILE * pr + hd
                s_ref[slot, h] = lax.dot_general(qa_ref[h], kt, _NT, preferred_element_type=_F32)

    def absorb(kb, slot, masked):
        ks = pl.multiple_of(kb * t, t)
        ones = jnp.ones((t, LANES), _BF16)
        for h, (pr, hd) in enumerate(heads):
            vt = jnp.concatenate([v_ref[pl.ds(ks, t), lanes_of(pr)], ones], axis=1)
            s = s_ref[slot, h]
            if masked:
                causal = (lax.broadcasted_iota(jnp.int32, (t, t), 1)
                          <= lax.broadcasted_iota(jnp.int32, (t, t), 0))
                s = jnp.where(causal, s, MASK_VALUE)
            m = m_ref[h]
            m_new = jnp.maximum(m, jnp.max(s, axis=-1, keepdims=True))
            p = jnp.exp2(s - jnp.concatenate([m_new] * (t // LANES), axis=1)).astype(_BF16)
            alpha = jnp.exp2(m - m_new)
            acc_ref[h] = (jnp.concatenate([alpha, alpha], axis=1) * acc_ref[h]
                          + jnp.dot(p, vt, preferred_element_type=_F32))
            m_ref[h] = m_new

    scores(0, 0)

    def two_blocks(kb):
        scores(kb + 1, 1)
        absorb(kb, 0, False)
        scores(kb + 2, 0)
        absorb(kb + 1, 1, False)

    def four_blocks(j, carry):
        two_blocks(4 * j)
        two_blocks(4 * j + 2)
        return carry

    lax.fori_loop(0, lax.shift_right_logical(i, 2), four_blocks, 0)

    @pl.when((i & 2) != 0)
    def _():
        two_blocks(i & ~3)

    @pl.when((i & 1) == 1)
    def _():
        scores(i, 1)
        absorb(i - 1, 0, False)
        absorb(i, 1, True)

    @pl.when((i & 1) == 0)
    def _():
        absorb(i, 0, True)

    for pr in range(FOX_PAIRS):
        outs = [acc_ref[HEADS_PER_TILE * pr + hd][:, :LANES] / acc_ref[HEADS_PER_TILE * pr + hd][:, LANES:]
                for hd in range(HEADS_PER_TILE)]
        o_ref[:, lanes_of(pr)] = jnp.where(low, outs[0], outs[1]).astype(_BF16)


def _fox_attention(qkv, ka, batch, seq_len):
    t = qkv.shape[0]
    hd = N_HEADS * HEAD_DIM
    n_steps = N_HEADS // (HEADS_PER_TILE * FOX_PAIRS)
    width = FOX_PAIRS * LANES
    heads = FOX_PAIRS * HEADS_PER_TILE
    nq = seq_len // FOX_T
    return pl.pallas_call(
        _fox_kernel,
        grid=(batch, n_steps, nq),
        in_specs=[
            pl.BlockSpec((FOX_T, width), lambda b, p, i: (b * nq + i, p)),
            pl.BlockSpec((seq_len, width), lambda b, p, i: (b, n_steps + p)),
            pl.BlockSpec((seq_len, width), lambda b, p, i: (b, p)),
            pl.BlockSpec((seq_len, width), lambda b, p, i: (b, 2 * n_steps + p)),
        ],
        out_specs=pl.BlockSpec((FOX_T, width), lambda b, p, i: (b * nq + i, p)),
        out_shape=jax.ShapeDtypeStruct((t, hd), _BF16),
        scratch_shapes=[
            pltpu.VMEM((heads, FOX_T, 2 * LANES), _BF16),
            pltpu.VMEM((2, heads, FOX_T, FOX_T), _F32),
            pltpu.VMEM((heads, FOX_T, LANES), _F32),
            pltpu.VMEM((heads, FOX_T, 2 * LANES), _F32),
        ],
        compiler_params=_params(("arbitrary", "arbitrary", "arbitrary")),
        name="fox_attn",
    )(qkv, qkv, ka, qkv)


def _out_kernel(a_ref, w_ref, x_ref, g1_ref, g2_ref, xo_ref, h_ref):
    half = a_ref.shape[0] // 2
    for r in (slice(0, half), slice(half, 2 * half)):
        y = jnp.dot(a_ref[r, :], w_ref[...], preferred_element_type=_F32)
        xn = x_ref[r, :] + y * _rms_scale(y) * g1_ref[...]
        xo_ref[r, :] = xn
        h_ref[r, :] = (xn * _rms_scale(xn) * g2_ref[...]).astype(_BF16)


def _out_proj(attn, w_all, idx, x2, g_post, g_pre_ffn):
    t, d = x2.shape
    k = attn.shape[1]
    row = lambda i: (i, 0)
    fixed = lambda i: (0, 0)
    return pl.pallas_call(
        _out_kernel,
        grid=(t // OUT_TM,),
        in_specs=[
            pl.BlockSpec((OUT_TM, k), row),
            pl.BlockSpec((None, k, d), lambda i: (idx, 0, 0)),
            pl.BlockSpec((OUT_TM, d), row),
            pl.BlockSpec((1, d), fixed),
            pl.BlockSpec((1, d), fixed),
        ],
        out_specs=[pl.BlockSpec((OUT_TM, d), row), pl.BlockSpec((OUT_TM, d), row)],
        out_shape=[jax.ShapeDtypeStruct((t, d), _F32), jax.ShapeDtypeStruct((t, d), _BF16)],
        compiler_params=_params(("arbitrary",)),
        name="out_proj",
    )(attn, w_all, x2, g_post, g_pre_ffn)


def _ffn_kernel(h_ref, wg_ref, wu_ref, wd_ref, x_ref, g_ref, o_ref, acc_ref):
    f = pl.program_id(1)

    @pl.when(f == 0)
    def _():
        acc_ref[...] = jnp.zeros_like(acc_ref)

    h = h_ref[...]
    gate = jnp.dot(h, wg_ref[...], preferred_element_type=_F32)
    up = jnp.dot(h, wu_ref[...], preferred_element_type=_F32)
    act = (gate * (1.0 / (1.0 + jnp.exp(-gate))) * up).astype(_BF16)
    acc_ref[...] += jnp.dot(act, wd_ref[...], preferred_element_type=_F32)

    @pl.when(f == pl.num_programs(1) - 1)
    def _():
        y = acc_ref[...]
        o_ref[...] = x_ref[...] + y * _rms_scale(y) * g_ref[...]


def _ffn(h, w_gu_all, w_down_all, idx, x2, g_post):
    t, d = x2.shape
    d_ff = w_down_all.shape[1]
    nf = d_ff // FFN_TF
    return pl.pallas_call(
        _ffn_kernel,
        grid=(t // FFN_TM, nf),
        in_specs=[
            pl.BlockSpec((FFN_TM, d), lambda i, f: (i, 0)),
            pl.BlockSpec((None, d, FFN_TF), lambda i, f: (idx, 0, f)),
            pl.BlockSpec((None, d, FFN_TF), lambda i, f: (idx, 0, nf + f)),
            pl.BlockSpec((None, FFN_TF, d), lambda i, f: (idx, f, 0)),
            pl.BlockSpec((FFN_TM, d), lambda i, f: (i, 0)),
            pl.BlockSpec((1, d), lambda i, f: (0, 0)),
        ],
        out_specs=pl.BlockSpec((FFN_TM, d), lambda i, f: (i, 0)),
        out_shape=jax.ShapeDtypeStruct((t, d), _F32),
        scratch_shapes=[pltpu.VMEM((FFN_TM, d), _F32)],
        compiler_params=_params(("arbitrary", "arbitrary")),
        name="ffn",
    )(h, w_gu_all, w_gu_all, w_down_all, x2, g_post)


def _rotary_lane_tables(positions):
    half = ROT_DIM // 2
    inv_freq = ROPE_THETA ** (-jnp.arange(0, ROT_DIM, 2, dtype=_F32) / ROT_DIM)
    ang = positions.astype(_F32).reshape(-1, 1) * inv_freq
    cos, sin = jnp.cos(ang), jnp.sin(ang)
    t = ang.shape[0]
    pad = HEAD_DIM - ROT_DIM
    ra = jnp.concatenate([cos, cos, jnp.ones((t, pad), _F32)], axis=1)
    rb = jnp.concatenate([-sin, jnp.zeros((t, half + pad), _F32)], axis=1)
    rc = jnp.concatenate([jnp.zeros((t, half), _F32), sin, jnp.zeros((t, pad), _F32)], axis=1)
    tile = lambda a: jnp.tile(a, (1, HEADS_PER_TILE))
    return tile(ra), tile(rb), tile(rc)


def kernel(x, positions, norm_gains, swa_w_in, swa_sinks, swa_w_out, fox_w_in, fox_b_f,
           fox_w_out, ffn_w_gate_up, ffn_w_down):
    batch, seq_len, d = x.shape
    depth = norm_gains.shape[0]
    hd = N_HEADS * HEAD_DIM
    assert d == D_MODEL and seq_len % PROJ_TM == 0 and seq_len % FOX_T == 0
    assert WINDOW <= BLOCK

    x2 = x.reshape(batch * seq_len, d)
    ra, rb, rc = _rotary_lane_tables(positions)
    gains = norm_gains.reshape(depth, 4, 1, d)
    swa_w_in_b, swa_w_out_b = swa_w_in.astype(_BF16), swa_w_out.astype(_BF16)
    fox_w_in_t = jnp.swapaxes(fox_w_in, 1, 2)
    fox_w_in_b, fox_w_out_b = fox_w_in_t.astype(_BF16), fox_w_out.astype(_BF16)
    ffn_w_gu_b, ffn_w_down_b = ffn_w_gate_up.astype(_BF16), ffn_w_down.astype(_BF16)

    for layer in range(depth):
        j = layer // 2
        g = gains[layer]
        if layer % 2 == 0:
            qkv = _proj_swa(x2, g[0], swa_w_in_b, j, ra, rb, rc)
            attn = _swa_attention(qkv, swa_sinks[j].astype(_F32), batch, seq_len)
            w_out = swa_w_out_b
        else:
            wf_pad = jnp.pad(fox_w_in_t[j, 3 * hd:, :].astype(_F32), ((0, LANES - N_HEADS), (0, 0)))
            wf_hi = wf_pad.astype(_BF16)
            wf_lo = (wf_pad - wf_hi.astype(_F32)).astype(_BF16)
            bf_pad = jnp.pad(fox_b_f[j].astype(_F32), (0, LANES - N_HEADS)).reshape(1, LANES)
            qkv, ka = _proj_fox(x2, g[0], fox_w_in_b, j,
                                jnp.concatenate([wf_hi, wf_lo], axis=0), bf_pad, seq_len)
            attn = _fox_attention(qkv, ka, batch, seq_len)
            w_out = fox_w_out_b
        x2, h = _out_proj(attn, w_out, j, x2, g[1], g[2])
        x2 = _ffn(h, ffn_w_gu_b, ffn_w_down_b, layer, x2, g[3])
    return x2.reshape(batch, seq_len, d)
```

```python
import functools

import jax
import jax.numpy as jnp
from jax import lax
from jax.experimental import pallas as pl
from jax.experimental.pallas import tpu as pltpu

D_MODEL = 2048
HEAD_DIM = 64
N_HEADS = D_MODEL // HEAD_DIM
SWA_KV_HEADS = 8
SWA_GROUP = N_HEADS // SWA_KV_HEADS
WINDOW = 128
BLOCK = 128
ROPE_THETA = 500000.0
ROT_DIM = HEAD_DIM // 4
RMS_EPS = 1e-6

LANES = 128
HEADS_PER_TILE = LANES // HEAD_DIM
VMEM_LIMIT_BYTES = 56 * 1024 * 1024
MASK_VALUE = -1e30

PROJ_TM = 1024
PROJ_TN = 512
OUT_TM = 512
FFN_TM = 512
FFN_TF = 512
FOX_T = 512
SWA_TILES = 2
FOX_PAIRS = 2
BIAS_TERMS = 3
LOG2E = 1.4426950408889634

_F32 = jnp.float32
_BF16 = jnp.bfloat16
_NT = (((1,), (1,)), ((), ()))


def _params(semantics):
    return pltpu.CompilerParams(dimension_semantics=semantics,
                                vmem_limit_bytes=VMEM_LIMIT_BYTES)


def _rms_scale(v):
    return lax.rsqrt(jnp.mean(v * v, axis=-1, keepdims=True) + RMS_EPS)


def _proj_swa_kernel(x_ref, g_ref, w_ref, ra_ref, rb_ref, rc_ref, o_ref, h_ref,
                     *, n_q_blocks, n_rope_blocks):
    j = pl.program_id(1)

    @pl.when(j == 0)
    def _():
        x = x_ref[...]
        h_ref[...] = (x * _rms_scale(x) * g_ref[...]).astype(_BF16)

    res = jnp.dot(h_ref[...], w_ref[...], preferred_element_type=_F32)

    rope = (j < n_rope_blocks).astype(_F32)
    scale = jnp.where(j < n_q_blocks, HEAD_DIM ** -0.5 * LOG2E, 1.0).astype(_F32)
    ra = (ra_ref[...] * rope + (1.0 - rope)) * scale
    rb = rb_ref[...] * (rope * scale)
    rc = rc_ref[...] * (rope * scale)
    half = ROT_DIM // 2
    for c in range(res.shape[1] // LANES):
        r = res[:, c * LANES:(c + 1) * LANES]
        rot = r * ra + pltpu.roll(r, LANES - half, 1) * rb + pltpu.roll(r, half, 1) * rc
        o_ref[:, c * LANES:(c + 1) * LANES] = rot.astype(_BF16)


def _proj_swa(x2, gain, w_all, idx, ra, rb, rc):
    t, d = x2.shape
    n = w_all.shape[2]
    qd = N_HEADS * HEAD_DIM
    kd = SWA_KV_HEADS * HEAD_DIM
    kern = functools.partial(_proj_swa_kernel, n_q_blocks=qd // PROJ_TN,
                             n_rope_blocks=(qd + kd) // PROJ_TN)
    return pl.pallas_call(
        kern,
        grid=(t // PROJ_TM, n // PROJ_TN),
        in_specs=[
            pl.BlockSpec((PROJ_TM, d), lambda i, j: (i, 0)),
            pl.BlockSpec((1, d), lambda i, j: (0, 0)),
            pl.BlockSpec((None, d, PROJ_TN), lambda i, j: (idx, 0, j)),
            pl.BlockSpec((PROJ_TM, LANES), lambda i, j: (i, 0)),
            pl.BlockSpec((PROJ_TM, LANES), lambda i, j: (i, 0)),
            pl.BlockSpec((PROJ_TM, LANES), lambda i, j: (i, 0)),
        ],
        out_specs=pl.BlockSpec((PROJ_TM, PROJ_TN), lambda i, j: (i, j)),
        out_shape=jax.ShapeDtypeStruct((t, n), _BF16),
        scratch_shapes=[pltpu.VMEM((PROJ_TM, d), _BF16)],
        compiler_params=_params(("arbitrary", "arbitrary")),
        name="proj_swa",
    )(x2, gain, w_all, ra, rb, rc)


def _pack_terms(v, lane):
    t0 = v.astype(_BF16).astype(_F32)
    r1 = v - t0
    t1 = r1.astype(_BF16).astype(_F32)
    t2 = (r1 - t1).astype(_BF16).astype(_F32)
    packed = jnp.where(lane < N_HEADS, t0,
                       jnp.where(lane < 2 * N_HEADS, pltpu.roll(t1, N_HEADS, 1),
                                 jnp.where(lane < 3 * N_HEADS, pltpu.roll(t2, 2 * N_HEADS, 1), 0.0)))
    return packed.astype(_BF16)


def _proj_fox_kernel(x_ref, g_ref, w_ref, wf_ref, bf_ref, tri_ref, sel_ref, o_ref, ka_ref,
                     h_ref, carry_ref, *, n_q_blocks, blocks_per_seq):
    i = pl.program_id(0)
    j = pl.program_id(1)

    @pl.when(j == 0)
    def _():
        x = x_ref[...]
        hf = x * _rms_scale(x) * g_ref[...]
        h_hi = hf.astype(_BF16)
        h_ref[...] = h_hi
        h_lo = (hf - h_hi.astype(_F32)).astype(_BF16)
        both = lax.dot_general(h_hi, wf_ref[...], _NT, preferred_element_type=_F32)
        logit = (both[:, :LANES] + both[:, LANES:]
                 + lax.dot_general(h_lo, wf_ref[:LANES, :], _NT,
                                   preferred_element_type=_F32))
        z = logit + bf_ref[...]
        log_f = jnp.minimum(z, 0.0) - jnp.log1p(jnp.exp(-jnp.abs(z)))
        lane = lax.broadcasted_iota(jnp.int32, log_f.shape, 1)
        part = jnp.dot(tri_ref[...], _pack_terms(log_f, lane), preferred_element_type=_F32)
        csum = (part + pltpu.roll(part, LANES - N_HEADS, 1)
                + pltpu.roll(part, LANES - 2 * N_HEADS, 1))

        @pl.when(i % blocks_per_seq == 0)
        def _():
            carry_ref[...] = jnp.zeros_like(carry_ref)

        c = csum + carry_ref[...]
        carry_ref[...] = c[PROJ_TM - 1:PROJ_TM, :]
        ka = jnp.dot(_pack_terms(c * -LOG2E, lane), sel_ref[...], preferred_element_type=_F32)
        ka_ref[...] = ka.astype(_BF16)

    res = lax.dot_general(h_ref[...], w_ref[...], _NT, preferred_element_type=_F32)
    scale = jnp.where(j < n_q_blocks, HEAD_DIM ** -0.5 * LOG2E, 1.0).astype(_F32)
    o_ref[...] = (res * scale).astype(_BF16)


def _bias_lane_selector():
    n_pairs = N_HEADS // HEADS_PER_TILE
    row = jnp.arange(LANES)[:, None]
    s, h = row // N_HEADS, row % N_HEADS
    col = jnp.arange(n_pairs * LANES)[None, :]
    target = (h // HEADS_PER_TILE) * LANES + BIAS_TERMS * (h % HEADS_PER_TILE) + s
    return ((col == target) & (s < BIAS_TERMS)).astype(_BF16)


def _proj_fox(x2, gain, w_all, idx, wf_hi_lo, bf_pad, seq_len):
    t, d = x2.shape
    n = 3 * N_HEADS * HEAD_DIM
    ka_cols = (N_HEADS // HEADS_PER_TILE) * LANES
    tri = (lax.broadcasted_iota(jnp.int32, (PROJ_TM, PROJ_TM), 1)
           <= lax.broadcasted_iota(jnp.int32, (PROJ_TM, PROJ_TM), 0)).astype(_BF16)
    kern = functools.partial(_proj_fox_kernel,
                             n_q_blocks=(N_HEADS * HEAD_DIM) // PROJ_TN,
                             blocks_per_seq=seq_len // PROJ_TM)
    return pl.pallas_call(
        kern,
        grid=(t // PROJ_TM, n // PROJ_TN),
        in_specs=[
            pl.BlockSpec((PROJ_TM, d), lambda i, j: (i, 0)),
            pl.BlockSpec((1, d), lambda i, j: (0, 0)),
            pl.BlockSpec((None, PROJ_TN, d), lambda i, j: (idx, j, 0)),
            pl.BlockSpec((2 * LANES, d), lambda i, j: (0, 0)),
            pl.BlockSpec((1, LANES), lambda i, j: (0, 0)),
            pl.BlockSpec((PROJ_TM, PROJ_TM), lambda i, j: (0, 0)),
            pl.BlockSpec((LANES, ka_cols), lambda i, j: (0, 0)),
        ],
        out_specs=[
            pl.BlockSpec((PROJ_TM, PROJ_TN), lambda i, j: (i, j)),
            pl.BlockSpec((PROJ_TM, ka_cols), lambda i, j: (i, 0)),
        ],
        out_shape=[jax.ShapeDtypeStruct((t, n), _BF16),
                   jax.ShapeDtypeStruct((t, ka_cols), _BF16)],
        scratch_shapes=[pltpu.VMEM((PROJ_TM, d), _BF16),
                        pltpu.VMEM((1, LANES), _F32)],
        compiler_params=_params(("arbitrary", "arbitrary")),
        name="proj_fox",
    )(x2, gain, w_all, wf_hi_lo, bf_pad, tri, _bias_lane_selector())


def _swa_kernel(sink_ref, q_ref, kp_ref, kc_ref, vp_ref, vc_ref, o_ref):
    g = pl.program_id(1)
    n = pl.program_id(2)
    lane = lax.broadcasted_iota(jnp.int32, (1, LANES), 1)
    low = lane < HEAD_DIM

    def both_halves(prev_ref, cur_ref, tile):
        cols = slice(tile * LANES, (tile + 1) * LANES)
        band = jnp.concatenate([prev_ref[:, cols], cur_ref[:, cols]], axis=0).astype(_F32)
        swapped = pltpu.roll(band, HEAD_DIM, 1)
        return (jnp.where(low, band, swapped).astype(_BF16),
                jnp.where(low, swapped, band).astype(_BF16))

    q_pos = lax.broadcasted_iota(jnp.int32, (BLOCK, 2 * BLOCK), 0)
    s_pos = lax.broadcasted_iota(jnp.int32, (BLOCK, 2 * BLOCK), 1)
    rel = BLOCK + q_pos - s_pos
    valid = (rel >= 0) & (rel < WINDOW) & ((s_pos >= BLOCK) | (n > 0))

    zero = jnp.zeros((), _BF16)
    chunks_per_kv = SWA_GROUP // HEADS_PER_TILE
    dups = [(both_halves(kp_ref, kc_ref, tile), both_halves(vp_ref, vc_ref, tile))
            for tile in range(SWA_TILES)]
    for tile, kv in [(tile, kv) for tile in range(SWA_TILES) for kv in range(HEADS_PER_TILE)]:
        k_dup, v_dup = dups[tile]
        kv_head = tile * HEADS_PER_TILE + kv
        chunks = [q_ref[:, (chunks_per_kv * kv_head + c) * LANES:(chunks_per_kv * kv_head + c + 1) * LANES]
                  for c in range(chunks_per_kv)]
        q_stack = jnp.concatenate(
            [jnp.where(low if hh == 0 else ~low, qc, zero)
             for qc in chunks for hh in range(HEADS_PER_TILE)], axis=0)
        s_all = lax.dot_general(q_stack, k_dup[kv], _NT, preferred_element_type=_F32)
        probs, sink_terms = [], []
        for h in range(SWA_GROUP):
            s = jnp.where(valid, s_all[h * BLOCK:(h + 1) * BLOCK], MASK_VALUE)
            sink = sink_ref[(g * SWA_TILES * HEADS_PER_TILE + kv_head) * SWA_GROUP + h] * LOG2E
            m = jnp.maximum(jnp.max(s, axis=-1, keepdims=True), sink)
            probs.append(jnp.exp2(s - m).astype(_BF16))
            sink_terms.append(jnp.exp2(sink - m))
        v_ones = jnp.concatenate([v_dup[kv], jnp.ones((2 * BLOCK, LANES), _BF16)], axis=1)
        o_all = jnp.dot(jnp.concatenate(probs, axis=0), v_ones,
                        preferred_element_type=_F32)

        def normalised(h):
            rows = slice(h * BLOCK, (h + 1) * BLOCK)
            return o_all[rows, :LANES] / (o_all[rows, LANES:] + sink_terms[h])

        for c in range(chunks_per_kv):
            o0 = normalised(HEADS_PER_TILE * c)
            o1 = normalised(HEADS_PER_TILE * c + 1)
            col = (chunks_per_kv * kv_head + c) * LANES
            o_ref[:, col:col + LANES] = jnp.where(low, o0, o1).astype(_BF16)


def _swa_attention(qkv, sinks, batch, seq_len):
    t = qkv.shape[0]
    nblk = seq_len // BLOCK
    qd = N_HEADS * HEAD_DIM
    kd = SWA_KV_HEADS * HEAD_DIM
    kv_tile = SWA_TILES * LANES
    q_tile = kv_tile * SWA_GROUP
    k_col0 = qd // kv_tile
    v_col0 = (qd + kd) // kv_tile

    def cur(col0):
        return lambda b, g, n: (b * nblk + n, col0 + g)

    def prev(col0):
        return lambda b, g, n: (b * nblk + jnp.maximum(n - 1, 0), col0 + g)

    return pl.pallas_call(
        _swa_kernel,
        grid=(batch, kd // kv_tile, nblk),
        in_specs=[
            pl.BlockSpec(memory_space=pltpu.SMEM),
            pl.BlockSpec((BLOCK, q_tile), lambda b, g, n: (b * nblk + n, g)),
            pl.BlockSpec((BLOCK, kv_tile), prev(k_col0)),
            pl.BlockSpec((BLOCK, kv_tile), cur(k_col0)),
            pl.BlockSpec((BLOCK, kv_tile), prev(v_col0)),
            pl.BlockSpec((BLOCK, kv_tile), cur(v_col0)),
        ],
        out_specs=pl.BlockSpec((BLOCK, q_tile), lambda b, g, n: (b * nblk + n, g)),
        out_shape=jax.ShapeDtypeStruct((t, qd), _BF16),
        compiler_params=_params(("arbitrary", "arbitrary", "arbitrary")),
        name="swa_attn",
    )(sinks, qkv, qkv, qkv, qkv, qkv)


def _fox_kernel(q_ref, k_ref, ka_ref, v_ref, o_ref, qa_ref, s_ref, m_ref, acc_ref):
    i = pl.program_id(2)
    t = FOX_T
    lane = lax.broadcasted_iota(jnp.int32, (1, LANES), 1)
    low = lane < HEAD_DIM
    zero = jnp.zeros((), _BF16)
    heads = [(pr, hd) for pr in range(FOX_PAIRS) for hd in range(HEADS_PER_TILE)]

    def lanes_of(pr):
        return slice(pr * LANES, (pr + 1) * LANES)

    for h, (pr, hd) in enumerate(heads):
        own = low if hd == 0 else ~low
        bias_lanes = (lane >= BIAS_TERMS * hd) & (lane < BIAS_TERMS * (hd + 1))
        pick = jnp.broadcast_to(jnp.where(bias_lanes, 1.0, 0.0).astype(_BF16), (t, LANES))
        qa_ref[h] = jnp.concatenate([jnp.where(own, q_ref[:, lanes_of(pr)], zero), pick], axis=1)
        m_ref[h] = jnp.full((t, LANES), MASK_VALUE, _F32)
        acc_ref[h] = jnp.zeros((t, 2 * LANES), _F32)

    def scores(kb, slot):
        ks = pl.multiple_of(kb * t, t)
        for pr in range(FOX_PAIRS):
            kt = jnp.concatenate([k_ref[pl.ds(ks, t), lanes_of(pr)],
                                  ka_ref[pl.ds(ks, t), lanes_of(pr)]], axis=1)
            for hd in range(HEADS_PER_TILE):
                h = HEADS_PER_TILE * pr + hd
                s_ref[slot, h] = lax.dot_general(qa_ref[h], kt, _NT, preferred_element_type=_F32)

    def absorb(kb, slot, masked):
        ks = pl.multiple_of(kb * t, t)
        ones = jnp.ones((t, LANES), _BF16)
        for h, (pr, hd) in enumerate(heads):
            vt = jnp.concatenate([v_ref[pl.ds(ks, t), lanes_of(pr)], ones], axis=1)
            s = s_ref[slot, h]
            if masked:
                causal = (lax.broadcasted_iota(jnp.int32, (t, t), 1)
                          <= lax.broadcasted_iota(jnp.int32, (t, t), 0))
                s = jnp.where(causal, s, MASK_VALUE)
            m = m_ref[h]
            m_new = jnp.maximum(m, jnp.max(s, axis=-1, keepdims=True))
            p = jnp.exp2(s - jnp.concatenate([m_new] * (t // LANES), axis=1)).astype(_BF16)
            alpha = jnp.exp2(m - m_new)
            acc_ref[h] = (jnp.concatenate([alpha, alpha], axis=1) * acc_ref[h]
                          + jnp.dot(p, vt, preferred_element_type=_F32))
            m_ref[h] = m_new

    scores(0, 0)

    def two_blocks(kb):
        scores(kb + 1, 1)
        absorb(kb, 0, False)
        scores(kb + 2, 0)
        absorb(kb + 1, 1, False)

    def four_blocks(j, carry):
        two_blocks(4 * j)
        two_blocks(4 * j + 2)
        return carry

    lax.fori_loop(0, lax.shift_right_logical(i, 2), four_blocks, 0)

    @pl.when((i & 2) != 0)
    def _():
        two_blocks(i & ~3)

    @pl.when((i & 1) == 1)
    def _():
        scores(i, 1)
        absorb(i - 1, 0, False)
        absorb(i, 1, True)

    @pl.when((i & 1) == 0)
    def _():
        absorb(i, 0, True)

    for pr in range(FOX_PAIRS):
        outs = [acc_ref[HEADS_PER_TILE * pr + hd][:, :LANES] / acc_ref[HEADS_PER_TILE * pr + hd][:, LANES:]
                for hd in range(HEADS_PER_TILE)]
        o_ref[:, lanes_of(pr)] = jnp.where(low, outs[0], outs[1]).astype(_BF16)


def _fox_attention(qkv, ka, batch, seq_len):
    t = qkv.shape[0]
    hd = N_HEADS * HEAD_DIM
    n_steps = N_HEADS // (HEADS_PER_TILE * FOX_PAIRS)
    width = FOX_PAIRS * LANES
    heads = FOX_PAIRS * HEADS_PER_TILE
    nq = seq_len // FOX_T
    return pl.pallas_call(
        _fox_kernel,
        grid=(batch, n_steps, nq),
        in_specs=[
            pl.BlockSpec((FOX_T, width), lambda b, p, i: (b * nq + i, p)),
            pl.BlockSpec((seq_len, width), lambda b, p, i: (b, n_steps + p)),
            pl.BlockSpec((seq_len, width), lambda b, p, i: (b, p)),
            pl.BlockSpec((seq_len, width), lambda b, p, i: (b, 2 * n_steps + p)),
        ],
        out_specs=pl.BlockSpec((FOX_T, width), lambda b, p, i: (b * nq + i, p)),
        out_shape=jax.ShapeDtypeStruct((t, hd), _BF16),
        scratch_shapes=[
            pltpu.VMEM((heads, FOX_T, 2 * LANES), _BF16),
            pltpu.VMEM((2, heads, FOX_T, FOX_T), _F32),
            pltpu.VMEM((heads, FOX_T, LANES), _F32),
            pltpu.VMEM((heads, FOX_T, 2 * LANES), _F32),
        ],
        compiler_params=_params(("arbitrary", "arbitrary", "arbitrary")),
        name="fox_attn",
    )(qkv, qkv, ka, qkv)


def _out_kernel(a_ref, w_ref, x_ref, g1_ref, g2_ref, xo_ref, h_ref):
    half = a_ref.shape[0] // 2
    for r in (slice(0, half), slice(half, 2 * half)):
        y = jnp.dot(a_ref[r, :], w_ref[...], preferred_element_type=_F32)
        xn = x_ref[r, :] + y * _rms_scale(y) * g1_ref[...]
        xo_ref[r, :] = xn
        h_ref[r, :] = (xn * _rms_scale(xn) * g2_ref[...]).astype(_BF16)


def _out_proj(attn, w_all, idx, x2, g_post, g_pre_ffn):
    t, d = x2.shape
    k = attn.shape[1]
    row = lambda i: (i, 0)
    fixed = lambda i: (0, 0)
    return pl.pallas_call(
        _out_kernel,
        grid=(t // OUT_TM,),
        in_specs=[
            pl.BlockSpec((OUT_TM, k), row),
            pl.BlockSpec((None, k, d), lambda i: (idx, 0, 0)),
            pl.BlockSpec((OUT_TM, d), row),
            pl.BlockSpec((1, d), fixed),
            pl.BlockSpec((1, d), fixed),
        ],
        out_specs=[pl.BlockSpec((OUT_TM, d), row), pl.BlockSpec((OUT_TM, d), row)],
        out_shape=[jax.ShapeDtypeStruct((t, d), _F32), jax.ShapeDtypeStruct((t, d), _BF16)],
        compiler_params=_params(("arbitrary",)),
        name="out_proj",
    )(attn, w_all, x2, g_post, g_pre_ffn)


def _ffn_kernel(h_ref, wg_ref, wu_ref, wd_ref, x_ref, g_ref, o_ref, acc_ref, *, nf, n_rows):
    s = pl.program_id(0)
    i = s // nf
    f = s - i * nf

    def chunk():
        h = h_ref[...]
        gate = jnp.dot(h, wg_ref[...], preferred_element_type=_F32)
        up = jnp.dot(h, wu_ref[...], preferred_element_type=_F32)
        act = (gate * (1.0 / (1.0 + jnp.exp(-gate))) * up).astype(_BF16)
        return jnp.dot(act, wd_ref[...], preferred_element_type=_F32)

    def finish_previous():
        y = acc_ref[...]
        o_ref[...] = x_ref[...] + y * _rms_scale(y) * g_ref[...]

    @pl.when(f > 0)
    def _():
        acc_ref[...] += chunk()

    @pl.when((f == 0) & (i == 0))
    def _():
        acc_ref[...] = chunk()

    @pl.when((f == 0) & (i > 0) & (i < n_rows))
    def _():
        finish_previous()
        acc_ref[...] = chunk()

    @pl.when(i == n_rows)
    def _():
        finish_previous()


def _ffn(h, w_gu_all, w_down_all, idx, x2, g_post):
    t, d = x2.shape
    d_ff = w_down_all.shape[1]
    nf = d_ff // FFN_TF
    n_rows = t // FFN_TM
    last = n_rows * nf

    def row(s):
        return jnp.minimum(s // nf, n_rows - 1)

    def col(s):
        return jnp.where(s < last, s % nf, nf - 1)

    def prev_row(s):
        return jnp.maximum(s // nf - 1, 0)

    kern = functools.partial(_ffn_kernel, nf=nf, n_rows=n_rows)
    return pl.pallas_call(
        kern,
        grid=(last + 1,),
        in_specs=[
            pl.BlockSpec((FFN_TM, d), lambda s: (row(s), 0)),
            pl.BlockSpec((None, d, FFN_TF), lambda s: (idx, 0, col(s))),
            pl.BlockSpec((None, d, FFN_TF), lambda s: (idx, 0, nf + col(s))),
            pl.BlockSpec((None, FFN_TF, d), lambda s: (idx, col(s), 0)),
            pl.BlockSpec((FFN_TM, d), lambda s: (prev_row(s), 0)),
            pl.BlockSpec((1, d), lambda s: (0, 0)),
        ],
        out_specs=pl.BlockSpec((FFN_TM, d), lambda s: (prev_row(s), 0)),
        out_shape=jax.ShapeDtypeStruct((t, d), _F32),
        scratch_shapes=[pltpu.VMEM((FFN_TM, d), _F32)],
        compiler_params=_params(("arbitrary",)),
        name="ffn",
    )(h, w_gu_all, w_gu_all, w_down_all, x2, g_post)


def _rotary_lane_tables(positions):
    half = ROT_DIM // 2
    inv_freq = ROPE_THETA ** (-jnp.arange(0, ROT_DIM, 2, dtype=_F32) / ROT_DIM)
    ang = positions.astype(_F32).reshape(-1, 1) * inv_freq
    cos, sin = jnp.cos(ang), jnp.sin(ang)
    t = ang.shape[0]
    pad = HEAD_DIM - ROT_DIM
    ra = jnp.concatenate([cos, cos, jnp.ones((t, pad), _F32)], axis=1)
    rb = jnp.concatenate([-sin, jnp.zeros((t, half + pad), _F32)], axis=1)
    rc = jnp.concatenate([jnp.zeros((t, half), _F32), sin, jnp.zeros((t, pad), _F32)], axis=1)
    tile = lambda a: jnp.tile(a, (1, HEADS_PER_TILE))
    return tile(ra), tile(rb), tile(rc)


def kernel(x, positions, norm_gains, swa_w_in, swa_sinks, swa_w_out, fox_w_in, fox_b_f,
           fox_w_out, ffn_w_gate_up, ffn_w_down):
    batch, seq_len, d = x.shape
    depth = norm_gains.shape[0]
    hd = N_HEADS * HEAD_DIM
    assert d == D_MODEL and seq_len % PROJ_TM == 0 and seq_len % FOX_T == 0
    assert WINDOW <= BLOCK

    x2 = x.reshape(batch * seq_len, d)
    ra, rb, rc = _rotary_lane_tables(positions)
    gains = norm_gains.reshape(depth, 4, 1, d)
    swa_w_in_b, swa_w_out_b = swa_w_in.astype(_BF16), swa_w_out.astype(_BF16)
    fox_w_in_t = jnp.swapaxes(fox_w_in, 1, 2)
    fox_w_in_b, fox_w_out_b = fox_w_in_t.astype(_BF16), fox_w_out.astype(_BF16)
    ffn_w_gu_b, ffn_w_down_b = ffn_w_gate_up.astype(_BF16), ffn_w_down.astype(_BF16)

    for layer in range(depth):
        j = layer // 2
        g = gains[layer]
        if layer % 2 == 0:
            qkv = _proj_swa(x2, g[0], swa_w_in_b, j, ra, rb, rc)
            attn = _swa_attention(qkv, swa_sinks[j].astype(_F32), batch, seq_len)
            w_out = swa_w_out_b
        else:
            wf_pad = jnp.pad(fox_w_in_t[j, 3 * hd:, :].astype(_F32), ((0, LANES - N_HEADS), (0, 0)))
            wf_hi = wf_pad.astype(_BF16)
            wf_lo = (wf_pad - wf_hi.astype(_F32)).astype(_BF16)
            bf_pad = jnp.pad(fox_b_f[j].astype(_F32), (0, LANES - N_HEADS)).reshape(1, LANES)
            qkv, ka = _proj_fox(x2, g[0], fox_w_in_b, j,
                                jnp.concatenate([wf_hi, wf_lo], axis=0), bf_pad, seq_len)
            attn = _fox_attention(qkv, ka, batch, seq_len)
            w_out = fox_w_out_b
        x2, h = _out_proj(attn, w_out, j, x2, g[1], g[2])
        x2 = _ffn(h, ffn_w_gu_b, ffn_w_down_b, layer, x2, g[3])
    return x2.reshape(batch, seq_len, d)
```

```python
import functools

import jax
import jax.numpy as jnp
from jax import lax
from jax.experimental import pallas as pl
from jax.experimental.pallas import tpu as pltpu

D_MODEL = 2048
HEAD_DIM = 64
N_HEADS = D_MODEL // HEAD_DIM
SWA_KV_HEADS = 8
SWA_GROUP = N_HEADS // SWA_KV_HEADS
WINDOW = 128
BLOCK = 128
ROPE_THETA = 500000.0
ROT_DIM = HEAD_DIM // 4
RMS_EPS = 1e-6

LANES = 128
HEADS_PER_TILE = LANES // HEAD_DIM
VMEM_LIMIT_BYTES = 56 * 1024 * 1024
MASK_VALUE = -1e30

PROJ_TM = 1024
PROJ_TN = 512
OUT_TM = 512
FFN_TM = 512
FFN_TF = 512
FOX_T = 512
SWA_TILES = 2
FOX_PAIRS = 2
BIAS_TERMS = 3
LOG2E = 1.4426950408889634

_F32 = jnp.float32
_BF16 = jnp.bfloat16
_NT = (((1,), (1,)), ((), ()))


def _params(semantics):
    return pltpu.CompilerParams(dimension_semantics=semantics,
                                vmem_limit_bytes=VMEM_LIMIT_BYTES)


def _rms_scale(v):
    return lax.rsqrt(jnp.mean(v * v, axis=-1, keepdims=True) + RMS_EPS)


def _proj_swa_kernel(x_ref, g_ref, w_ref, ra_ref, rb_ref, rc_ref, o_ref, h_ref,
                     *, n_q_blocks, n_rope_blocks):
    j = pl.program_id(1)

    @pl.when(j == 0)
    def _():
        x = x_ref[...]
        h_ref[...] = (x * _rms_scale(x) * g_ref[...]).astype(_BF16)

    res = jnp.dot(h_ref[...], w_ref[...], preferred_element_type=_F32)

    rope = (j < n_rope_blocks).astype(_F32)
    scale = jnp.where(j < n_q_blocks, HEAD_DIM ** -0.5 * LOG2E, 1.0).astype(_F32)
    ra = (ra_ref[...] * rope + (1.0 - rope)) * scale
    rb = rb_ref[...] * (rope * scale)
    rc = rc_ref[...] * (rope * scale)
    half = ROT_DIM // 2
    for c in range(res.shape[1] // LANES):
        r = res[:, c * LANES:(c + 1) * LANES]
        rot = r * ra + pltpu.roll(r, LANES - half, 1) * rb + pltpu.roll(r, half, 1) * rc
        o_ref[:, c * LANES:(c + 1) * LANES] = rot.astype(_BF16)


def _proj_swa(x2, gain, w_all, idx, ra, rb, rc):
    t, d = x2.shape
    n = w_all.shape[1] * PROJ_TN
    qd = N_HEADS * HEAD_DIM
    kd = SWA_KV_HEADS * HEAD_DIM
    kern = functools.partial(_proj_swa_kernel, n_q_blocks=qd // PROJ_TN,
                             n_rope_blocks=(qd + kd) // PROJ_TN)
    return pl.pallas_call(
        kern,
        grid=(t // PROJ_TM, n // PROJ_TN),
        in_specs=[
            pl.BlockSpec((PROJ_TM, d), lambda i, j: (i, 0)),
            pl.BlockSpec((1, d), lambda i, j: (0, 0)),
            pl.BlockSpec((None, None, d, PROJ_TN), lambda i, j: (idx, j, 0, 0)),
            pl.BlockSpec((PROJ_TM, LANES), lambda i, j: (i, 0)),
            pl.BlockSpec((PROJ_TM, LANES), lambda i, j: (i, 0)),
            pl.BlockSpec((PROJ_TM, LANES), lambda i, j: (i, 0)),
        ],
        out_specs=pl.BlockSpec((PROJ_TM, PROJ_TN), lambda i, j: (i, j)),
        out_shape=jax.ShapeDtypeStruct((t, n), _BF16),
        scratch_shapes=[pltpu.VMEM((PROJ_TM, d), _BF16)],
        compiler_params=_params(("arbitrary", "arbitrary")),
        name="proj_swa",
    )(x2, gain, w_all, ra, rb, rc)


def _pack_terms(v, lane):
    t0 = v.astype(_BF16).astype(_F32)
    r1 = v - t0
    t1 = r1.astype(_BF16).astype(_F32)
    t2 = (r1 - t1).astype(_BF16).astype(_F32)
    packed = jnp.where(lane < N_HEADS, t0,
                       jnp.where(lane < 2 * N_HEADS, pltpu.roll(t1, N_HEADS, 1),
                                 jnp.where(lane < 3 * N_HEADS, pltpu.roll(t2, 2 * N_HEADS, 1), 0.0)))
    return packed.astype(_BF16)


def _proj_fox_kernel(x_ref, g_ref, w_ref, wf_ref, bf_ref, tri_ref, sel_ref, o_ref, ka_ref,
                     h_ref, carry_ref, *, n_q_blocks, blocks_per_seq):
    i = pl.program_id(0)
    j = pl.program_id(1)

    @pl.when(j == 0)
    def _():
        x = x_ref[...]
        hf = x * _rms_scale(x) * g_ref[...]
        h_hi = hf.astype(_BF16)
        h_ref[...] = h_hi
        h_lo = (hf - h_hi.astype(_F32)).astype(_BF16)
        both = lax.dot_general(h_hi, wf_ref[...], _NT, preferred_element_type=_F32)
        logit = (both[:, :LANES] + both[:, LANES:]
                 + lax.dot_general(h_lo, wf_ref[:LANES, :], _NT,
                                   preferred_element_type=_F32))
        z = logit + bf_ref[...]
        log_f = jnp.minimum(z, 0.0) - jnp.log1p(jnp.exp(-jnp.abs(z)))
        lane = lax.broadcasted_iota(jnp.int32, log_f.shape, 1)
        part = jnp.dot(tri_ref[...], _pack_terms(log_f, lane), preferred_element_type=_F32)
        csum = (part + pltpu.roll(part, LANES - N_HEADS, 1)
                + pltpu.roll(part, LANES - 2 * N_HEADS, 1))

        @pl.when(i % blocks_per_seq == 0)
        def _():
            carry_ref[...] = jnp.zeros_like(carry_ref)

        c = csum + carry_ref[...]
        carry_ref[...] = c[PROJ_TM - 1:PROJ_TM, :]
        ka = jnp.dot(_pack_terms(c * -LOG2E, lane), sel_ref[...], preferred_element_type=_F32)
        ka_ref[...] = ka.astype(_BF16)

    res = lax.dot_general(h_ref[...], w_ref[...], _NT, preferred_element_type=_F32)
    scale = jnp.where(j < n_q_blocks, HEAD_DIM ** -0.5 * LOG2E, 1.0).astype(_F32)
    o_ref[...] = (res * scale).astype(_BF16)


def _bias_lane_selector():
    n_pairs = N_HEADS // HEADS_PER_TILE
    row = jnp.arange(LANES)[:, None]
    s, h = row // N_HEADS, row % N_HEADS
    col = jnp.arange(n_pairs * LANES)[None, :]
    target = (h // HEADS_PER_TILE) * LANES + BIAS_TERMS * (h % HEADS_PER_TILE) + s
    return ((col == target) & (s < BIAS_TERMS)).astype(_BF16)


def _proj_fox(x2, gain, w_all, idx, wf_hi_lo, bf_pad, seq_len):
    t, d = x2.shape
    n = 3 * N_HEADS * HEAD_DIM
    ka_cols = (N_HEADS // HEADS_PER_TILE) * LANES
    tri = (lax.broadcasted_iota(jnp.int32, (PROJ_TM, PROJ_TM), 1)
           <= lax.broadcasted_iota(jnp.int32, (PROJ_TM, PROJ_TM), 0)).astype(_BF16)
    kern = functools.partial(_proj_fox_kernel,
                             n_q_blocks=(N_HEADS * HEAD_DIM) // PROJ_TN,
                             blocks_per_seq=seq_len // PROJ_TM)
    return pl.pallas_call(
        kern,
        grid=(t // PROJ_TM, n // PROJ_TN),
        in_specs=[
            pl.BlockSpec((PROJ_TM, d), lambda i, j: (i, 0)),
            pl.BlockSpec((1, d), lambda i, j: (0, 0)),
            pl.BlockSpec((None, PROJ_TN, d), lambda i, j: (idx, j, 0)),
            pl.BlockSpec((2 * LANES, d), lambda i, j: (0, 0)),
            pl.BlockSpec((1, LANES), lambda i, j: (0, 0)),
            pl.BlockSpec((PROJ_TM, PROJ_TM), lambda i, j: (0, 0)),
            pl.BlockSpec((LANES, ka_cols), lambda i, j: (0, 0)),
        ],
        out_specs=[
            pl.BlockSpec((PROJ_TM, PROJ_TN), lambda i, j: (i, j)),
            pl.BlockSpec((PROJ_TM, ka_cols), lambda i, j: (i, 0)),
        ],
        out_shape=[jax.ShapeDtypeStruct((t, n), _BF16),
                   jax.ShapeDtypeStruct((t, ka_cols), _BF16)],
        scratch_shapes=[pltpu.VMEM((PROJ_TM, d), _BF16),
                        pltpu.VMEM((1, LANES), _F32)],
        compiler_params=_params(("arbitrary", "arbitrary")),
        name="proj_fox",
    )(x2, gain, w_all, wf_hi_lo, bf_pad, tri, _bias_lane_selector())


def _swa_kernel(sink_ref, q_ref, kp_ref, kc_ref, vp_ref, vc_ref, o_ref):
    g = pl.program_id(1)
    n = pl.program_id(2)
    lane = lax.broadcasted_iota(jnp.int32, (1, LANES), 1)
    low = lane < HEAD_DIM

    def both_halves(prev_ref, cur_ref, tile):
        cols = slice(tile * LANES, (tile + 1) * LANES)
        band = jnp.concatenate([prev_ref[:, cols], cur_ref[:, cols]], axis=0).astype(_F32)
        swapped = pltpu.roll(band, HEAD_DIM, 1)
        return (jnp.where(low, band, swapped).astype(_BF16),
                jnp.where(low, swapped, band).astype(_BF16))

    q_pos = lax.broadcasted_iota(jnp.int32, (BLOCK, 2 * BLOCK), 0)
    s_pos = lax.broadcasted_iota(jnp.int32, (BLOCK, 2 * BLOCK), 1)
    rel = BLOCK + q_pos - s_pos
    valid = (rel >= 0) & (rel < WINDOW) & ((s_pos >= BLOCK) | (n > 0))

    zero = jnp.zeros((), _BF16)
    chunks_per_kv = SWA_GROUP // HEADS_PER_TILE
    dups = [(both_halves(kp_ref, kc_ref, tile), both_halves(vp_ref, vc_ref, tile))
            for tile in range(SWA_TILES)]
    for tile, kv in [(tile, kv) for tile in range(SWA_TILES) for kv in range(HEADS_PER_TILE)]:
        k_dup, v_dup = dups[tile]
        kv_head = tile * HEADS_PER_TILE + kv
        chunks = [q_ref[:, (chunks_per_kv * kv_head + c) * LANES:(chunks_per_kv * kv_head + c + 1) * LANES]
                  for c in range(chunks_per_kv)]
        q_stack = jnp.concatenate(
            [jnp.where(low if hh == 0 else ~low, qc, zero)
             for qc in chunks for hh in range(HEADS_PER_TILE)], axis=0)
        s_all = lax.dot_general(q_stack, k_dup[kv], _NT, preferred_element_type=_F32)
        probs, sink_terms = [], []
        for h in range(SWA_GROUP):
            s = jnp.where(valid, s_all[h * BLOCK:(h + 1) * BLOCK], MASK_VALUE)
            sink = sink_ref[(g * SWA_TILES * HEADS_PER_TILE + kv_head) * SWA_GROUP + h] * LOG2E
            m = jnp.maximum(jnp.max(s, axis=-1, keepdims=True), sink)
            probs.append(jnp.exp2(s - m).astype(_BF16))
            sink_terms.append(jnp.exp2(sink - m))
        v_ones = jnp.concatenate([v_dup[kv], jnp.ones((2 * BLOCK, LANES), _BF16)], axis=1)
        o_all = jnp.dot(jnp.concatenate(probs, axis=0), v_ones,
                        preferred_element_type=_F32)

        def normalised(h):
            rows = slice(h * BLOCK, (h + 1) * BLOCK)
            return o_all[rows, :LANES] / (o_all[rows, LANES:] + sink_terms[h])

        for c in range(chunks_per_kv):
            o0 = normalised(HEADS_PER_TILE * c)
            o1 = normalised(HEADS_PER_TILE * c + 1)
            col = (chunks_per_kv * kv_head + c) * LANES
            o_ref[:, col:col + LANES] = jnp.where(low, o0, o1).astype(_BF16)


def _swa_attention(qkv, sinks, batch, seq_len):
    t = qkv.shape[0]
    nblk = seq_len // BLOCK
    qd = N_HEADS * HEAD_DIM
    kd = SWA_KV_HEADS * HEAD_DIM
    kv_tile = SWA_TILES * LANES
    q_tile = kv_tile * SWA_GROUP
    k_col0 = qd // kv_tile
    v_col0 = (qd + kd) // kv_tile

    def cur(col0):
        return lambda b, g, n: (b * nblk + n, col0 + g)

    def prev(col0):
        return lambda b, g, n: (b * nblk + jnp.maximum(n - 1, 0), col0 + g)

    return pl.pallas_call(
        _swa_kernel,
        grid=(batch, kd // kv_tile, nblk),
        in_specs=[
            pl.BlockSpec(memory_space=pltpu.SMEM),
            pl.BlockSpec((BLOCK, q_tile), lambda b, g, n: (b * nblk + n, g)),
            pl.BlockSpec((BLOCK, kv_tile), prev(k_col0)),
            pl.BlockSpec((BLOCK, kv_tile), cur(k_col0)),
            pl.BlockSpec((BLOCK, kv_tile), prev(v_col0)),
            pl.BlockSpec((BLOCK, kv_tile), cur(v_col0)),
        ],
        out_specs=pl.BlockSpec((BLOCK, q_tile), lambda b, g, n: (b * nblk + n, g)),
        out_shape=jax.ShapeDtypeStruct((t, qd), _BF16),
        compiler_params=_params(("arbitrary", "arbitrary", "arbitrary")),
        name="swa_attn",
    )(sinks, qkv, qkv, qkv, qkv, qkv)


def _fox_kernel(q_ref, k_ref, ka_ref, v_ref, o_ref, qa_ref, s_ref, m_ref, acc_ref):
    i = pl.program_id(2)
    t = FOX_T
    lane = lax.broadcasted_iota(jnp.int32, (1, LANES), 1)
    low = lane < HEAD_DIM
    zero = jnp.zeros((), _BF16)
    heads = [(pr, hd) for pr in range(FOX_PAIRS) for hd in range(HEADS_PER_TILE)]

    def lanes_of(pr):
        return slice(pr * LANES, (pr + 1) * LANES)

    for h, (pr, hd) in enumerate(heads):
        own = low if hd == 0 else ~low
        bias_lanes = (lane >= BIAS_TERMS * hd) & (lane < BIAS_TERMS * (hd + 1))
        pick = jnp.broadcast_to(jnp.where(bias_lanes, 1.0, 0.0).astype(_BF16), (t, LANES))
        qa_ref[h] = jnp.concatenate([jnp.where(own, q_ref[:, lanes_of(pr)], zero), pick], axis=1)
        m_ref[h] = jnp.full((t, LANES), MASK_VALUE, _F32)
        acc_ref[h] = jnp.zeros((t, 2 * LANES), _F32)

    def scores(kb, slot):
        ks = pl.multiple_of(kb * t, t)
        for pr in range(FOX_PAIRS):
            kt = jnp.concatenate([k_ref[pl.ds(ks, t), lanes_of(pr)],
                                  ka_ref[pl.ds(ks, t), lanes_of(pr)]], axis=1)
            for hd in range(HEADS_PER_TILE):
                h = HEADS_PER_TILE * pr + hd
                s_ref[slot, h] = lax.dot_general(qa_ref[h], kt, _NT, preferred_element_type=_F32)

    def absorb(kb, slot, masked):
        ks = pl.multiple_of(kb * t, t)
        ones = jnp.ones((t, LANES), _BF16)
        for h, (pr, hd) in enumerate(heads):
            vt = jnp.concatenate([v_ref[pl.ds(ks, t), lanes_of(pr)], ones], axis=1)
            s = s_ref[slot, h]
            if masked:
                causal = (lax.broadcasted_iota(jnp.int32, (t, t), 1)
                          <= lax.broadcasted_iota(jnp.int32, (t, t), 0))
                s = jnp.where(causal, s, MASK_VALUE)
            m = m_ref[h]
            m_new = jnp.maximum(m, jnp.max(s, axis=-1, keepdims=True))
            p = jnp.exp2(s - jnp.concatenate([m_new] * (t // LANES), axis=1)).astype(_BF16)
            alpha = jnp.exp2(m - m_new)
            acc_ref[h] = (jnp.concatenate([alpha, alpha], axis=1) * acc_ref[h]
                          + jnp.dot(p, vt, preferred_element_type=_F32))
            m_ref[h] = m_new

    scores(0, 0)

    def two_blocks(kb):
        scores(kb + 1, 1)
        absorb(kb, 0, False)
        scores(kb + 2, 0)
        absorb(kb + 1, 1, False)

    def four_blocks(j, carry):
        two_blocks(4 * j)
        two_blocks(4 * j + 2)
        return carry

    lax.fori_loop(0, lax.shift_right_logical(i, 2), four_blocks, 0)

    @pl.when((i & 2) != 0)
    def _():
        two_blocks(i & ~3)

    @pl.when((i & 1) == 1)
    def _():
        scores(i, 1)
        absorb(i - 1, 0, False)
        absorb(i, 1, True)

    @pl.when((i & 1) == 0)
    def _():
        absorb(i, 0, True)

    for pr in range(FOX_PAIRS):
        outs = [acc_ref[HEADS_PER_TILE * pr + hd][:, :LANES] / acc_ref[HEADS_PER_TILE * pr + hd][:, LANES:]
                for hd in range(HEADS_PER_TILE)]
        o_ref[:, lanes_of(pr)] = jnp.where(low, outs[0], outs[1]).astype(_BF16)


def _fox_attention(qkv, ka, batch, seq_len):
    t = qkv.shape[0]
    hd = N_HEADS * HEAD_DIM
    n_steps = N_HEADS // (HEADS_PER_TILE * FOX_PAIRS)
    width = FOX_PAIRS * LANES
    heads = FOX_PAIRS * HEADS_PER_TILE
    nq = seq_len // FOX_T
    return pl.pallas_call(
        _fox_kernel,
        grid=(batch, n_steps, nq),
        in_specs=[
            pl.BlockSpec((FOX_T, width), lambda b, p, i: (b * nq + i, p)),
            pl.BlockSpec((seq_len, width), lambda b, p, i: (b, n_steps + p)),
            pl.BlockSpec((seq_len, width), lambda b, p, i: (b, p)),
            pl.BlockSpec((seq_len, width), lambda b, p, i: (b, 2 * n_steps + p)),
        ],
        out_specs=pl.BlockSpec((FOX_T, width), lambda b, p, i: (b * nq + i, p)),
        out_shape=jax.ShapeDtypeStruct((t, hd), _BF16),
        scratch_shapes=[
            pltpu.VMEM((heads, FOX_T, 2 * LANES), _BF16),
            pltpu.VMEM((2, heads, FOX_T, FOX_T), _F32),
            pltpu.VMEM((heads, FOX_T, LANES), _F32),
            pltpu.VMEM((heads, FOX_T, 2 * LANES), _F32),
        ],
        compiler_params=_params(("arbitrary", "arbitrary", "arbitrary")),
        name="fox_attn",
    )(qkv, qkv, ka, qkv)


def _out_kernel(a_ref, w_ref, x_ref, g1_ref, g2_ref, xo_ref, h_ref):
    half = a_ref.shape[0] // 2
    for r in (slice(0, half), slice(half, 2 * half)):
        y = jnp.dot(a_ref[r, :], w_ref[...], preferred_element_type=_F32)
        xn = x_ref[r, :] + y * _rms_scale(y) * g1_ref[...]
        xo_ref[r, :] = xn
        h_ref[r, :] = (xn * _rms_scale(xn) * g2_ref[...]).astype(_BF16)


def _out_proj(attn, w_all, idx, x2, g_post, g_pre_ffn):
    t, d = x2.shape
    k = attn.shape[1]
    row = lambda i: (i, 0)
    fixed = lambda i: (0, 0)
    return pl.pallas_call(
        _out_kernel,
        grid=(t // OUT_TM,),
        in_specs=[
            pl.BlockSpec((OUT_TM, k), row),
            pl.BlockSpec((None, k, d), lambda i: (idx, 0, 0)),
            pl.BlockSpec((OUT_TM, d), row),
            pl.BlockSpec((1, d), fixed),
            pl.BlockSpec((1, d), fixed),
        ],
        out_specs=[pl.BlockSpec((OUT_TM, d), row), pl.BlockSpec((OUT_TM, d), row)],
        out_shape=[jax.ShapeDtypeStruct((t, d), _F32), jax.ShapeDtypeStruct((t, d), _BF16)],
        compiler_params=_params(("arbitrary",)),
        name="out_proj",
    )(attn, w_all, x2, g_post, g_pre_ffn)


def _ffn_kernel(h_ref, wg_ref, wu_ref, wd_ref, x_ref, g_ref, o_ref, acc_ref, *, nf, n_rows):
    s = pl.program_id(0)
    i = s // nf
    f = s - i * nf

    def chunk():
        h = h_ref[...]
        gate = jnp.dot(h, wg_ref[...], preferred_element_type=_F32)
        up = jnp.dot(h, wu_ref[...], preferred_element_type=_F32)
        act = (gate * (1.0 / (1.0 + jnp.exp(-gate))) * up).astype(_BF16)
        return jnp.dot(act, wd_ref[...], preferred_element_type=_F32)

    def finish_previous():
        y = acc_ref[...]
        o_ref[...] = x_ref[...] + y * _rms_scale(y) * g_ref[...]

    @pl.when(f > 0)
    def _():
        acc_ref[...] += chunk()

    @pl.when((f == 0) & (i == 0))
    def _():
        acc_ref[...] = chunk()

    @pl.when((f == 0) & (i > 0) & (i < n_rows))
    def _():
        finish_previous()
        acc_ref[...] = chunk()

    @pl.when(i == n_rows)
    def _():
        finish_previous()


def _ffn(h, w_gu_all, w_down_all, idx, x2, g_post):
    t, d = x2.shape
    d_ff = w_down_all.shape[1]
    nf = d_ff // FFN_TF
    n_rows = t // FFN_TM
    last = n_rows * nf

    def row(s):
        return jnp.minimum(s // nf, n_rows - 1)

    def col(s):
        return jnp.where(s < last, s % nf, nf - 1)

    def prev_row(s):
        return jnp.maximum(s // nf - 1, 0)

    kern = functools.partial(_ffn_kernel, nf=nf, n_rows=n_rows)
    return pl.pallas_call(
        kern,
        grid=(last + 1,),
        in_specs=[
            pl.BlockSpec((FFN_TM, d), lambda s: (row(s), 0)),
            pl.BlockSpec((None, None, d, FFN_TF), lambda s: (idx, col(s), 0, 0)),
            pl.BlockSpec((None, None, d, FFN_TF), lambda s: (idx, nf + col(s), 0, 0)),
            pl.BlockSpec((None, FFN_TF, d), lambda s: (idx, col(s), 0)),
            pl.BlockSpec((FFN_TM, d), lambda s: (prev_row(s), 0)),
            pl.BlockSpec((1, d), lambda s: (0, 0)),
        ],
        out_specs=pl.BlockSpec((FFN_TM, d), lambda s: (prev_row(s), 0)),
        out_shape=jax.ShapeDtypeStruct((t, d), _F32),
        scratch_shapes=[pltpu.VMEM((FFN_TM, d), _F32)],
        compiler_params=_params(("arbitrary",)),
        name="ffn",
    )(h, w_gu_all, w_gu_all, w_down_all, x2, g_post)


def _rotary_lane_tables(positions):
    half = ROT_DIM // 2
    inv_freq = ROPE_THETA ** (-jnp.arange(0, ROT_DIM, 2, dtype=_F32) / ROT_DIM)
    ang = positions.astype(_F32).reshape(-1, 1) * inv_freq
    cos, sin = jnp.cos(ang), jnp.sin(ang)
    t = ang.shape[0]
    pad = HEAD_DIM - ROT_DIM
    ra = jnp.concatenate([cos, cos, jnp.ones((t, pad), _F32)], axis=1)
    rb = jnp.concatenate([-sin, jnp.zeros((t, half + pad), _F32)], axis=1)
    rc = jnp.concatenate([jnp.zeros((t, half), _F32), sin, jnp.zeros((t, pad), _F32)], axis=1)
    tile = lambda a: jnp.tile(a, (1, HEADS_PER_TILE))
    return tile(ra), tile(rb), tile(rc)


def _column_blocks(w, width):
    layers, d, n = w.shape
    return w.reshape(layers, d, n // width, width).transpose(0, 2, 1, 3)


def kernel(x, positions, norm_gains, swa_w_in, swa_sinks, swa_w_out, fox_w_in, fox_b_f,
           fox_w_out, ffn_w_gate_up, ffn_w_down):
    batch, seq_len, d = x.shape
    depth = norm_gains.shape[0]
    hd = N_HEADS * HEAD_DIM
    assert d == D_MODEL and seq_len % PROJ_TM == 0 and seq_len % FOX_T == 0
    assert WINDOW <= BLOCK

    x2 = x.reshape(batch * seq_len, d)
    ra, rb, rc = _rotary_lane_tables(positions)
    gains = norm_gains.reshape(depth, 4, 1, d)
    swa_w_in_b, swa_w_out_b = _column_blocks(swa_w_in.astype(_BF16), PROJ_TN), swa_w_out.astype(_BF16)
    fox_w_in_t = jnp.swapaxes(fox_w_in, 1, 2)
    fox_w_in_b, fox_w_out_b = fox_w_in_t.astype(_BF16), fox_w_out.astype(_BF16)
    ffn_w_gu_b = _column_blocks(ffn_w_gate_up.astype(_BF16), FFN_TF)
    ffn_w_down_b = ffn_w_down.astype(_BF16)

    for layer in range(depth):
        j = layer // 2
        g = gains[layer]
        if layer % 2 == 0:
            qkv = _proj_swa(x2, g[0], swa_w_in_b, j, ra, rb, rc)
            attn = _swa_attention(qkv, swa_sinks[j].astype(_F32), batch, seq_len)
            w_out = swa_w_out_b
        else:
            wf_pad = jnp.pad(fox_w_in_t[j, 3 * hd:, :].astype(_F32), ((0, LANES - N_HEADS), (0, 0)))
            wf_hi = wf_pad.astype(_BF16)
            wf_lo = (wf_pad - wf_hi.astype(_F32)).astype(_BF16)
            bf_pad = jnp.pad(fox_b_f[j].astype(_F32), (0, LANES - N_HEADS)).reshape(1, LANES)
            qkv, ka = _proj_fox(x2, g[0], fox_w_in_b, j,
                                jnp.concatenate([wf_hi, wf_lo], axis=0), bf_pad, seq_len)
            attn = _fox_attention(qkv, ka, batch, seq_len)
            w_out = fox_w_out_b
        x2, h = _out_proj(attn, w_out, j, x2, g[1], g[2])
        x2 = _ffn(h, ffn_w_gu_b, ffn_w_down_b, layer, x2, g[3])
    return x2.reshape(batch, seq_len, d)
```

```python
import functools

import jax
import jax.numpy as jnp
from jax import lax
from jax.experimental import pallas as pl
from jax.experimental.pallas import tpu as pltpu

D_MODEL = 2048
HEAD_DIM = 64
N_HEADS = D_MODEL // HEAD_DIM
SWA_KV_HEADS = 8
SWA_GROUP = N_HEADS // SWA_KV_HEADS
WINDOW = 128
BLOCK = 128
ROPE_THETA = 500000.0
ROT_DIM = HEAD_DIM // 4
RMS_EPS = 1e-6

LANES = 128
HEADS_PER_TILE = LANES // HEAD_DIM
VMEM_LIMIT_BYTES = 56 * 1024 * 1024
MASK_VALUE = -1e30

PROJ_TM = 1024
PROJ_TN = 512
SWA_PROJ_TN = 1024
OUT_TM = 512
FFN_TM = 512
FFN_TF = 512
FOX_T = 512
SWA_TILES = 2
FOX_PAIRS = 2
BIAS_TERMS = 3
LOG2E = 1.4426950408889634

_F32 = jnp.float32
_BF16 = jnp.bfloat16
_NT = (((1,), (1,)), ((), ()))


def _params(semantics):
    return pltpu.CompilerParams(dimension_semantics=semantics,
                                vmem_limit_bytes=VMEM_LIMIT_BYTES)


def _rms_scale(v):
    return lax.rsqrt(jnp.mean(v * v, axis=-1, keepdims=True) + RMS_EPS)


def _proj_swa_kernel(x_ref, g_ref, w_ref, ra_ref, rb_ref, rc_ref, o_ref, h_ref,
                     *, n_q_tiles, n_rope_tiles):
    j = pl.program_id(1)

    @pl.when(j == 0)
    def _():
        x = x_ref[...]
        h_ref[...] = (x * _rms_scale(x) * g_ref[...]).astype(_BF16)

    res = jnp.dot(h_ref[...], w_ref[...], preferred_element_type=_F32)

    half = ROT_DIM // 2
    tiles = res.shape[1] // LANES
    for c in range(tiles):
        tile = j * tiles + c
        rope = (tile < n_rope_tiles).astype(_F32)
        scale = jnp.where(tile < n_q_tiles, HEAD_DIM ** -0.5 * LOG2E, 1.0).astype(_F32)
        ra = (ra_ref[...] * rope + (1.0 - rope)) * scale
        rb = rb_ref[...] * (rope * scale)
        rc = rc_ref[...] * (rope * scale)
        r = res[:, c * LANES:(c + 1) * LANES]
        rot = r * ra + pltpu.roll(r, LANES - half, 1) * rb + pltpu.roll(r, half, 1) * rc
        o_ref[:, c * LANES:(c + 1) * LANES] = rot.astype(_BF16)


def _proj_swa(x2, gain, w_all, idx, ra, rb, rc):
    t, d = x2.shape
    n = w_all.shape[2]
    qd = N_HEADS * HEAD_DIM
    kd = SWA_KV_HEADS * HEAD_DIM
    kern = functools.partial(_proj_swa_kernel, n_q_tiles=qd // LANES,
                             n_rope_tiles=(qd + kd) // LANES)
    return pl.pallas_call(
        kern,
        grid=(t // PROJ_TM, n // SWA_PROJ_TN),
        in_specs=[
            pl.BlockSpec((PROJ_TM, d), lambda i, j: (i, 0)),
            pl.BlockSpec((1, d), lambda i, j: (0, 0)),
            pl.BlockSpec((None, d, SWA_PROJ_TN), lambda i, j: (idx, 0, j)),
            pl.BlockSpec((PROJ_TM, LANES), lambda i, j: (i, 0)),
            pl.BlockSpec((PROJ_TM, LANES), lambda i, j: (i, 0)),
            pl.BlockSpec((PROJ_TM, LANES), lambda i, j: (i, 0)),
        ],
        out_specs=pl.BlockSpec((PROJ_TM, SWA_PROJ_TN), lambda i, j: (i, j)),
        out_shape=jax.ShapeDtypeStruct((t, n), _BF16),
        scratch_shapes=[pltpu.VMEM((PROJ_TM, d), _BF16)],
        compiler_params=_params(("arbitrary", "arbitrary")),
        name="proj_swa",
    )(x2, gain, w_all, ra, rb, rc)


def _pack_terms(v, lane):
    t0 = v.astype(_BF16).astype(_F32)
    r1 = v - t0
    t1 = r1.astype(_BF16).astype(_F32)
    t2 = (r1 - t1).astype(_BF16).astype(_F32)
    packed = jnp.where(lane < N_HEADS, t0,
                       jnp.where(lane < 2 * N_HEADS, pltpu.roll(t1, N_HEADS, 1),
                                 jnp.where(lane < 3 * N_HEADS, pltpu.roll(t2, 2 * N_HEADS, 1), 0.0)))
    return packed.astype(_BF16)


def _proj_fox_kernel(x_ref, g_ref, w_ref, wf_ref, bf_ref, tri_ref, sel_ref, o_ref, ka_ref,
                     h_ref, carry_ref, *, n_q_blocks, blocks_per_seq):
    i = pl.program_id(0)
    j = pl.program_id(1)

    @pl.when(j == 0)
    def _():
        x = x_ref[...]
        hf = x * _rms_scale(x) * g_ref[...]
        h_hi = hf.astype(_BF16)
        h_ref[...] = h_hi
        h_lo = (hf - h_hi.astype(_F32)).astype(_BF16)
        both = lax.dot_general(h_hi, wf_ref[...], _NT, preferred_element_type=_F32)
        logit = (both[:, :LANES] + both[:, LANES:]
                 + lax.dot_general(h_lo, wf_ref[:LANES, :], _NT,
                                   preferred_element_type=_F32))
        z = logit + bf_ref[...]
        log_f = jnp.minimum(z, 0.0) - jnp.log1p(jnp.exp(-jnp.abs(z)))
        lane = lax.broadcasted_iota(jnp.int32, log_f.shape, 1)
        part = jnp.dot(tri_ref[...], _pack_terms(log_f, lane), preferred_element_type=_F32)
        csum = (part + pltpu.roll(part, LANES - N_HEADS, 1)
                + pltpu.roll(part, LANES - 2 * N_HEADS, 1))

        @pl.when(i % blocks_per_seq == 0)
        def _():
            carry_ref[...] = jnp.zeros_like(carry_ref)

        c = csum + carry_ref[...]
        carry_ref[...] = c[PROJ_TM - 1:PROJ_TM, :]
        ka = jnp.dot(_pack_terms(c * -LOG2E, lane), sel_ref[...], preferred_element_type=_F32)
        ka_ref[...] = ka.astype(_BF16)

    res = lax.dot_general(h_ref[...], w_ref[...], _NT, preferred_element_type=_F32)
    scale = jnp.where(j < n_q_blocks, HEAD_DIM ** -0.5 * LOG2E, 1.0).astype(_F32)
    o_ref[...] = (res * scale).astype(_BF16)


def _bias_lane_selector():
    n_pairs = N_HEADS // HEADS_PER_TILE
    row = jnp.arange(LANES)[:, None]
    s, h = row // N_HEADS, row % N_HEADS
    col = jnp.arange(n_pairs * LANES)[None, :]
    target = (h // HEADS_PER_TILE) * LANES + BIAS_TERMS * (h % HEADS_PER_TILE) + s
    return ((col == target) & (s < BIAS_TERMS)).astype(_BF16)


def _proj_fox(x2, gain, w_all, idx, wf_hi_lo, bf_pad, seq_len):
    t, d = x2.shape
    n = 3 * N_HEADS * HEAD_DIM
    ka_cols = (N_HEADS // HEADS_PER_TILE) * LANES
    tri = (lax.broadcasted_iota(jnp.int32, (PROJ_TM, PROJ_TM), 1)
           <= lax.broadcasted_iota(jnp.int32, (PROJ_TM, PROJ_TM), 0)).astype(_BF16)
    kern = functools.partial(_proj_fox_kernel,
                             n_q_blocks=(N_HEADS * HEAD_DIM) // PROJ_TN,
                             blocks_per_seq=seq_len // PROJ_TM)
    return pl.pallas_call(
        kern,
        grid=(t // PROJ_TM, n // PROJ_TN),
        in_specs=[
            pl.BlockSpec((PROJ_TM, d), lambda i, j: (i, 0)),
            pl.BlockSpec((1, d), lambda i, j: (0, 0)),
            pl.BlockSpec((None, PROJ_TN, d), lambda i, j: (idx, j, 0)),
            pl.BlockSpec((2 * LANES, d), lambda i, j: (0, 0)),
            pl.BlockSpec((1, LANES), lambda i, j: (0, 0)),
            pl.BlockSpec((PROJ_TM, PROJ_TM), lambda i, j: (0, 0)),
            pl.BlockSpec((LANES, ka_cols), lambda i, j: (0, 0)),
        ],
        out_specs=[
            pl.BlockSpec((PROJ_TM, PROJ_TN), lambda i, j: (i, j)),
            pl.BlockSpec((PROJ_TM, ka_cols), lambda i, j: (i, 0)),
        ],
        out_shape=[jax.ShapeDtypeStruct((t, n), _BF16),
                   jax.ShapeDtypeStruct((t, ka_cols), _BF16)],
        scratch_shapes=[pltpu.VMEM((PROJ_TM, d), _BF16),
                        pltpu.VMEM((1, LANES), _F32)],
        compiler_params=_params(("arbitrary", "arbitrary")),
        name="proj_fox",
    )(x2, gain, w_all, wf_hi_lo, bf_pad, tri, _bias_lane_selector())


def _swa_kernel(sink_ref, q_ref, kp_ref, kc_ref, vp_ref, vc_ref, o_ref):
    g = pl.program_id(1)
    n = pl.program_id(2)
    lane = lax.broadcasted_iota(jnp.int32, (1, LANES), 1)
    low = lane < HEAD_DIM

    def both_halves(prev_ref, cur_ref, tile):
        cols = slice(tile * LANES, (tile + 1) * LANES)
        band = jnp.concatenate([prev_ref[:, cols], cur_ref[:, cols]], axis=0).astype(_F32)
        swapped = pltpu.roll(band, HEAD_DIM, 1)
        return (jnp.where(low, band, swapped).astype(_BF16),
                jnp.where(low, swapped, band).astype(_BF16))

    q_pos = lax.broadcasted_iota(jnp.int32, (BLOCK, 2 * BLOCK), 0)
    s_pos = lax.broadcasted_iota(jnp.int32, (BLOCK, 2 * BLOCK), 1)
    rel = BLOCK + q_pos - s_pos
    valid = (rel >= 0) & (rel < WINDOW) & ((s_pos >= BLOCK) | (n > 0))

    zero = jnp.zeros((), _BF16)
    chunks_per_kv = SWA_GROUP // HEADS_PER_TILE
    dups = [(both_halves(kp_ref, kc_ref, tile), both_halves(vp_ref, vc_ref, tile))
            for tile in range(SWA_TILES)]
    for tile, kv in [(tile, kv) for tile in range(SWA_TILES) for kv in range(HEADS_PER_TILE)]:
        k_dup, v_dup = dups[tile]
        kv_head = tile * HEADS_PER_TILE + kv
        chunks = [q_ref[:, (chunks_per_kv * kv_head + c) * LANES:(chunks_per_kv * kv_head + c + 1) * LANES]
                  for c in range(chunks_per_kv)]
        q_stack = jnp.concatenate(
            [jnp.where(low if hh == 0 else ~low, qc, zero)
             for qc in chunks for hh in range(HEADS_PER_TILE)], axis=0)
        s_all = lax.dot_general(q_stack, k_dup[kv], _NT, preferred_element_type=_F32)
        probs, sink_terms = [], []
        for h in range(SWA_GROUP):
            s = jnp.where(valid, s_all[h * BLOCK:(h + 1) * BLOCK], MASK_VALUE)
            sink = sink_ref[(g * SWA_TILES * HEADS_PER_TILE + kv_head) * SWA_GROUP + h] * LOG2E
            m = jnp.maximum(jnp.max(s, axis=-1, keepdims=True), sink)
            probs.append(jnp.exp2(s - m).astype(_BF16))
            sink_terms.append(jnp.exp2(sink - m))
        v_ones = jnp.concatenate([v_dup[kv], jnp.ones((2 * BLOCK, LANES), _BF16)], axis=1)
        o_all = jnp.dot(jnp.concatenate(probs, axis=0), v_ones,
                        preferred_element_type=_F32)

        def normalised(h):
            rows = slice(h * BLOCK, (h + 1) * BLOCK)
            return o_all[rows, :LANES] / (o_all[rows, LANES:] + sink_terms[h])

        for c in range(chunks_per_kv):
            o0 = normalised(HEADS_PER_TILE * c)
            o1 = normalised(HEADS_PER_TILE * c + 1)
            col = (chunks_per_kv * kv_head + c) * LANES
            o_ref[:, col:col + LANES] = jnp.where(low, o0, o1).astype(_BF16)


def _swa_attention(qkv, sinks, batch, seq_len):
    t = qkv.shape[0]
    nblk = seq_len // BLOCK
    qd = N_HEADS * HEAD_DIM
    kd = SWA_KV_HEADS * HEAD_DIM
    kv_tile = SWA_TILES * LANES
    q_tile = kv_tile * SWA_GROUP
    k_col0 = qd // kv_tile
    v_col0 = (qd + kd) // kv_tile

    def cur(col0):
        return lambda b, g, n: (b * nblk + n, col0 + g)

    def prev(col0):
        return lambda b, g, n: (b * nblk + jnp.maximum(n - 1, 0), col0 + g)

    return pl.pallas_call(
        _swa_kernel,
        grid=(batch, kd // kv_tile, nblk),
        in_specs=[
            pl.BlockSpec(memory_space=pltpu.SMEM),
            pl.BlockSpec((BLOCK, q_tile), lambda b, g, n: (b * nblk + n, g)),
            pl.BlockSpec((BLOCK, kv_tile), prev(k_col0)),
            pl.BlockSpec((BLOCK, kv_tile), cur(k_col0)),
            pl.BlockSpec((BLOCK, kv_tile), prev(v_col0)),
            pl.BlockSpec((BLOCK, kv_tile), cur(v_col0)),
        ],
        out_specs=pl.BlockSpec((BLOCK, q_tile), lambda b, g, n: (b * nblk + n, g)),
        out_shape=jax.ShapeDtypeStruct((t, qd), _BF16),
        compiler_params=_params(("arbitrary", "arbitrary", "arbitrary")),
        name="swa_attn",
    )(sinks, qkv, qkv, qkv, qkv, qkv)


def _fox_kernel(q_ref, k_ref, ka_ref, v_ref, o_ref, qa_ref, s_ref, m_ref, acc_ref):
    i = pl.program_id(2)
    t = FOX_T
    lane = lax.broadcasted_iota(jnp.int32, (1, LANES), 1)
    low = lane < HEAD_DIM
    zero = jnp.zeros((), _BF16)
    heads = [(pr, hd) for pr in range(FOX_PAIRS) for hd in range(HEADS_PER_TILE)]

    def lanes_of(pr):
        return slice(pr * LANES, (pr + 1) * LANES)

    for h, (pr, hd) in enumerate(heads):
        own = low if hd == 0 else ~low
        bias_lanes = (lane >= BIAS_TERMS * hd) & (lane < BIAS_TERMS * (hd + 1))
        pick = jnp.broadcast_to(jnp.where(bias_lanes, 1.0, 0.0).astype(_BF16), (t, LANES))
        qa_ref[h] = jnp.concatenate([jnp.where(own, q_ref[:, lanes_of(pr)], zero), pick], axis=1)
        m_ref[h] = jnp.full((t, LANES), MASK_VALUE, _F32)
        acc_ref[h] = jnp.zeros((t, 2 * LANES), _F32)

    def scores(kb, slot):
        ks = pl.multiple_of(kb * t, t)
        for pr in range(FOX_PAIRS):
            kt = jnp.concatenate([k_ref[pl.ds(ks, t), lanes_of(pr)],
                                  ka_ref[pl.ds(ks, t), lanes_of(pr)]], axis=1)
            for hd in range(HEADS_PER_TILE):
                h = HEADS_PER_TILE * pr + hd
                s_ref[slot, h] = lax.dot_general(qa_ref[h], kt, _NT, preferred_element_type=_F32)

    def absorb(kb, slot):
        ks = pl.multiple_of(kb * t, t)
        ones = jnp.ones((t, LANES), _BF16)
        for h, (pr, hd) in enumerate(heads):
            vt = jnp.concatenate([v_ref[pl.ds(ks, t), lanes_of(pr)], ones], axis=1)
            s = s_ref[slot, h]
            m = m_ref[h]
            m_new = jnp.maximum(m, jnp.max(s, axis=-1, keepdims=True))
            p = jnp.exp2(s - jnp.concatenate([m_new] * (t // LANES), axis=1)).astype(_BF16)
            alpha = jnp.exp2(m - m_new)
            acc_ref[h] = (jnp.concatenate([alpha, alpha], axis=1) * acc_ref[h]
                          + jnp.dot(p, vt, preferred_element_type=_F32))
            m_ref[h] = m_new

    def absorb_diagonal(kb, slot):
        ks = pl.multiple_of(kb * t, t)
        ones = jnp.ones((t, LANES), _BF16)
        for h, (pr, hd) in enumerate(heads):
            vt = jnp.concatenate([v_ref[pl.ds(ks, t), lanes_of(pr)], ones], axis=1)
            for r0, r1 in ((0, t // 2), (t // 2, t)):
                causal = (lax.broadcasted_iota(jnp.int32, (r1 - r0, r1), 1)
                          <= r0 + lax.broadcasted_iota(jnp.int32, (r1 - r0, r1), 0))
                s = jnp.where(causal, s_ref[slot, h, r0:r1, :r1], MASK_VALUE)
                m = m_ref[h, r0:r1, :]
                m_new = jnp.maximum(m, jnp.max(s, axis=-1, keepdims=True))
                p = jnp.exp2(s - jnp.concatenate([m_new] * (r1 // LANES), axis=1)).astype(_BF16)
                alpha = jnp.exp2(m - m_new)
                acc_ref[h, r0:r1, :] = (jnp.concatenate([alpha, alpha], axis=1) * acc_ref[h, r0:r1, :]
                                        + jnp.dot(p, vt[:r1], preferred_element_type=_F32))
                m_ref[h, r0:r1, :] = m_new

    scores(0, 0)

    def two_blocks(kb):
        scores(kb + 1, 1)
        absorb(kb, 0)
        scores(kb + 2, 0)
        absorb(kb + 1, 1)

    def four_blocks(j, carry):
        two_blocks(4 * j)
        two_blocks(4 * j + 2)
        return carry

    lax.fori_loop(0, lax.shift_right_logical(i, 2), four_blocks, 0)

    @pl.when((i & 2) != 0)
    def _():
        two_blocks(i & ~3)

    @pl.when((i & 1) == 1)
    def _():
        scores(i, 1)
        absorb(i - 1, 0)
        absorb_diagonal(i, 1)

    @pl.when((i & 1) == 0)
    def _():
        absorb_diagonal(i, 0)

    for pr in range(FOX_PAIRS):
        outs = [acc_ref[HEADS_PER_TILE * pr + hd][:, :LANES] / acc_ref[HEADS_PER_TILE * pr + hd][:, LANES:]
                for hd in range(HEADS_PER_TILE)]
        o_ref[:, lanes_of(pr)] = jnp.where(low, outs[0], outs[1]).astype(_BF16)


def _fox_attention(qkv, ka, batch, seq_len):
    t = qkv.shape[0]
    hd = N_HEADS * HEAD_DIM
    n_steps = N_HEADS // (HEADS_PER_TILE * FOX_PAIRS)
    width = FOX_PAIRS * LANES
    heads = FOX_PAIRS * HEADS_PER_TILE
    nq = seq_len // FOX_T
    return pl.pallas_call(
        _fox_kernel,
        grid=(batch, n_steps, nq),
        in_specs=[
            pl.BlockSpec((FOX_T, width), lambda b, p, i: (b * nq + i, p)),
            pl.BlockSpec((seq_len, width), lambda b, p, i: (b, n_steps + p)),
            pl.BlockSpec((seq_len, width), lambda b, p, i: (b, p)),
            pl.BlockSpec((seq_len, width), lambda b, p, i: (b, 2 * n_steps + p)),
        ],
        out_specs=pl.BlockSpec((FOX_T, width), lambda b, p, i: (b * nq + i, p)),
        out_shape=jax.ShapeDtypeStruct((t, hd), _BF16),
        scratch_shapes=[
            pltpu.VMEM((heads, FOX_T, 2 * LANES), _BF16),
            pltpu.VMEM((2, heads, FOX_T, FOX_T), _F32),
            pltpu.VMEM((heads, FOX_T, LANES), _F32),
            pltpu.VMEM((heads, FOX_T, 2 * LANES), _F32),
        ],
        compiler_params=_params(("arbitrary", "arbitrary", "arbitrary")),
        name="fox_attn",
    )(qkv, qkv, ka, qkv)


def _out_kernel(a_ref, w_ref, x_ref, g1_ref, g2_ref, xo_ref, h_ref):
    half = a_ref.shape[0] // 2
    for r in (slice(0, half), slice(half, 2 * half)):
        y = jnp.dot(a_ref[r, :], w_ref[...], preferred_element_type=_F32)
        xn = x_ref[r, :] + y * _rms_scale(y) * g1_ref[...]
        xo_ref[r, :] = xn
        h_ref[r, :] = (xn * _rms_scale(xn) * g2_ref[...]).astype(_BF16)


def _out_proj(attn, w_all, idx, x2, g_post, g_pre_ffn):
    t, d = x2.shape
    k = attn.shape[1]
    row = lambda i: (i, 0)
    fixed = lambda i: (0, 0)
    return pl.pallas_call(
        _out_kernel,
        grid=(t // OUT_TM,),
        in_specs=[
            pl.BlockSpec((OUT_TM, k), row),
            pl.BlockSpec((None, k, d), lambda i: (idx, 0, 0)),
            pl.BlockSpec((OUT_TM, d), row),
            pl.BlockSpec((1, d), fixed),
            pl.BlockSpec((1, d), fixed),
        ],
        out_specs=[pl.BlockSpec((OUT_TM, d), row), pl.BlockSpec((OUT_TM, d), row)],
        out_shape=[jax.ShapeDtypeStruct((t, d), _F32), jax.ShapeDtypeStruct((t, d), _BF16)],
        compiler_params=_params(("arbitrary",)),
        name="out_proj",
    )(attn, w_all, x2, g_post, g_pre_ffn)


def _ffn_kernel(h_ref, wg_ref, wu_ref, wd_ref, x_ref, g_ref, o_ref, acc_ref, *, nf, n_rows):
    s = pl.program_id(0)
    i = s // nf
    f = s - i * nf

    def chunk():
        h = h_ref[...]
        gate = jnp.dot(h, wg_ref[...], preferred_element_type=_F32)
        up = jnp.dot(h, wu_ref[...], preferred_element_type=_F32)
        act = (gate * (1.0 / (1.0 + jnp.exp(-gate))) * up).astype(_BF16)
        return jnp.dot(act, wd_ref[...], preferred_element_type=_F32)

    def finish_previous():
        y = acc_ref[...]
        o_ref[...] = x_ref[...] + y * _rms_scale(y) * g_ref[...]

    @pl.when(f > 0)
    def _():
        acc_ref[...] += chunk()

    @pl.when((f == 0) & (i == 0))
    def _():
        acc_ref[...] = chunk()

    @pl.when((f == 0) & (i > 0) & (i < n_rows))
    def _():
        finish_previous()
        acc_ref[...] = chunk()

    @pl.when(i == n_rows)
    def _():
        finish_previous()


def _ffn(h, w_gu_all, w_down_all, idx, x2, g_post):
    t, d = x2.shape
    d_ff = w_down_all.shape[1]
    nf = d_ff // FFN_TF
    n_rows = t // FFN_TM
    last = n_rows * nf

    def row(s):
        return jnp.minimum(s // nf, n_rows - 1)

    def col(s):
        return jnp.where(s < last, s % nf, nf - 1)

    def prev_row(s):
        return jnp.maximum(s // nf - 1, 0)

    kern = functools.partial(_ffn_kernel, nf=nf, n_rows=n_rows)
    return pl.pallas_call(
        kern,
        grid=(last + 1,),
        in_specs=[
            pl.BlockSpec((FFN_TM, d), lambda s: (row(s), 0)),
            pl.BlockSpec((None, d, FFN_TF), lambda s: (idx, 0, col(s))),
            pl.BlockSpec((None, d, FFN_TF), lambda s: (idx, 0, nf + col(s))),
            pl.BlockSpec((None, FFN_TF, d), lambda s: (idx, col(s), 0)),
            pl.BlockSpec((FFN_TM, d), lambda s: (prev_row(s), 0)),
            pl.BlockSpec((1, d), lambda s: (0, 0)),
        ],
        out_specs=pl.BlockSpec((FFN_TM, d), lambda s: (prev_row(s), 0)),
        out_shape=jax.ShapeDtypeStruct((t, d), _F32),
        scratch_shapes=[pltpu.VMEM((FFN_TM, d), _F32)],
        compiler_params=_params(("arbitrary",)),
        name="ffn",
    )(h, w_gu_all, w_gu_all, w_down_all, x2, g_post)


def _rotary_lane_tables(positions):
    half = ROT_DIM // 2
    inv_freq = ROPE_THETA ** (-jnp.arange(0, ROT_DIM, 2, dtype=_F32) / ROT_DIM)
    ang = positions.astype(_F32).reshape(-1, 1) * inv_freq
    cos, sin = jnp.cos(ang), jnp.sin(ang)
    t = ang.shape[0]
    pad = HEAD_DIM - ROT_DIM
    ra = jnp.concatenate([cos, cos, jnp.ones((t, pad), _F32)], axis=1)
    rb = jnp.concatenate([-sin, jnp.zeros((t, half + pad), _F32)], axis=1)
    rc = jnp.concatenate([jnp.zeros((t, half), _F32), sin, jnp.zeros((t, pad), _F32)], axis=1)
    tile = lambda a: jnp.tile(a, (1, HEADS_PER_TILE))
    return tile(ra), tile(rb), tile(rc)


def kernel(x, positions, norm_gains, swa_w_in, swa_sinks, swa_w_out, fox_w_in, fox_b_f,
           fox_w_out, ffn_w_gate_up, ffn_w_down):
    batch, seq_len, d = x.shape
    depth = norm_gains.shape[0]
    hd = N_HEADS * HEAD_DIM
    assert d == D_MODEL and seq_len % PROJ_TM == 0 and seq_len % FOX_T == 0
    assert WINDOW <= BLOCK

    x2 = x.reshape(batch * seq_len, d)
    ra, rb, rc = _rotary_lane_tables(positions)
    gains = norm_gains.reshape(depth, 4, 1, d)
    swa_w_in_b, swa_w_out_b = swa_w_in.astype(_BF16), swa_w_out.astype(_BF16)
    fox_w_in_t = jnp.swapaxes(fox_w_in, 1, 2)
    fox_w_in_b, fox_w_out_b = fox_w_in_t.astype(_BF16), fox_w_out.astype(_BF16)
    ffn_w_gu_b, ffn_w_down_b = ffn_w_gate_up.astype(_BF16), ffn_w_down.astype(_BF16)

    for layer in range(depth):
        j = layer // 2
        g = gains[layer]
        if layer % 2 == 0:
            qkv = _proj_swa(x2, g[0], swa_w_in_b, j, ra, rb, rc)
            attn = _swa_attention(qkv, swa_sinks[j].astype(_F32), batch, seq_len)
            w_out = swa_w_out_b
        else:
            wf_pad = jnp.pad(fox_w_in_t[j, 3 * hd:, :].astype(_F32), ((0, LANES - N_HEADS), (0, 0)))
            wf_hi = wf_pad.astype(_BF16)
            wf_lo = (wf_pad - wf_hi.astype(_F32)).astype(_BF16)
            bf_pad = jnp.pad(fox_b_f[j].astype(_F32), (0, LANES - N_HEADS)).reshape(1, LANES)
            qkv, ka = _proj_fox(x2, g[0], fox_w_in_b, j,
                                jnp.concatenate([wf_hi, wf_lo], axis=0), bf_pad, seq_len)
            attn = _fox_attention(qkv, ka, batch, seq_len)
            w_out = fox_w_out_b
        x2, h = _out_proj(attn, w_out, j, x2, g[1], g[2])
        x2 = _ffn(h, ffn_w_gu_b, ffn_w_down_b, layer, x2, g[3])
    return x2.reshape(batch, seq_len, d)
```

```python
import functools

import jax
import jax.numpy as jnp
from jax import lax
from jax.experimental import pallas as pl
from jax.experimental.pallas import tpu as pltpu

D_MODEL = 2048
HEAD_DIM = 64
N_HEADS = D_MODEL // HEAD_DIM
SWA_KV_HEADS = 8
SWA_GROUP = N_HEADS // SWA_KV_HEADS
WINDOW = 128
BLOCK = 128
ROPE_THETA = 500000.0
ROT_DIM = HEAD_DIM // 4
RMS_EPS = 1e-6

LANES = 128
HEADS_PER_TILE = LANES // HEAD_DIM
VMEM_LIMIT_BYTES = 56 * 1024 * 1024
MASK_VALUE = -1e30

PROJ_TM = 1024
PROJ_TN = 512
SWA_PROJ_TN = 1024
OUT_TM = 512
FFN_TM = 512
FFN_TF = 512
FOX_T = 512
SWA_TILES = 4
FOX_PAIRS = 2
BIAS_TERMS = 3
LOG2E = 1.4426950408889634

_F32 = jnp.float32
_BF16 = jnp.bfloat16
_NT = (((1,), (1,)), ((), ()))


def _params(semantics):
    return pltpu.CompilerParams(dimension_semantics=semantics,
                                vmem_limit_bytes=VMEM_LIMIT_BYTES)


def _rms_scale(v):
    return lax.rsqrt(jnp.mean(v * v, axis=-1, keepdims=True) + RMS_EPS)


def _proj_swa_kernel(x_ref, g_ref, w_ref, ra_ref, rb_ref, rc_ref, o_ref, h_ref,
                     *, n_q_tiles, n_rope_tiles):
    j = pl.program_id(1)

    @pl.when(j == 0)
    def _():
        x = x_ref[...]
        h_ref[...] = (x * _rms_scale(x) * g_ref[...]).astype(_BF16)

    res = jnp.dot(h_ref[...], w_ref[...], preferred_element_type=_F32)

    half = ROT_DIM // 2
    tiles = res.shape[1] // LANES
    for c in range(tiles):
        tile = j * tiles + c
        rope = (tile < n_rope_tiles).astype(_F32)
        scale = jnp.where(tile < n_q_tiles, HEAD_DIM ** -0.5 * LOG2E, 1.0).astype(_F32)
        ra = (ra_ref[...] * rope + (1.0 - rope)) * scale
        rb = rb_ref[...] * (rope * scale)
        rc = rc_ref[...] * (rope * scale)
        r = res[:, c * LANES:(c + 1) * LANES]
        rot = r * ra + pltpu.roll(r, LANES - half, 1) * rb + pltpu.roll(r, half, 1) * rc
        o_ref[:, c * LANES:(c + 1) * LANES] = rot.astype(_BF16)


def _proj_swa(x2, gain, w_all, idx, ra, rb, rc):
    t, d = x2.shape
    n = w_all.shape[2]
    qd = N_HEADS * HEAD_DIM
    kd = SWA_KV_HEADS * HEAD_DIM
    kern = functools.partial(_proj_swa_kernel, n_q_tiles=qd // LANES,
                             n_rope_tiles=(qd + kd) // LANES)
    return pl.pallas_call(
        kern,
        grid=(t // PROJ_TM, n // SWA_PROJ_TN),
        in_specs=[
            pl.BlockSpec((PROJ_TM, d), lambda i, j: (i, 0)),
            pl.BlockSpec((1, d), lambda i, j: (0, 0)),
            pl.BlockSpec((None, d, SWA_PROJ_TN), lambda i, j: (idx, 0, j)),
            pl.BlockSpec((PROJ_TM, LANES), lambda i, j: (i, 0)),
            pl.BlockSpec((PROJ_TM, LANES), lambda i, j: (i, 0)),
            pl.BlockSpec((PROJ_TM, LANES), lambda i, j: (i, 0)),
        ],
        out_specs=pl.BlockSpec((PROJ_TM, SWA_PROJ_TN), lambda i, j: (i, j)),
        out_shape=jax.ShapeDtypeStruct((t, n), _BF16),
        scratch_shapes=[pltpu.VMEM((PROJ_TM, d), _BF16)],
        compiler_params=_params(("arbitrary", "arbitrary")),
        name="proj_swa",
    )(x2, gain, w_all, ra, rb, rc)


def _pack_terms(v, lane):
    t0 = v.astype(_BF16).astype(_F32)
    r1 = v - t0
    t1 = r1.astype(_BF16).astype(_F32)
    t2 = (r1 - t1).astype(_BF16).astype(_F32)
    packed = jnp.where(lane < N_HEADS, t0,
                       jnp.where(lane < 2 * N_HEADS, pltpu.roll(t1, N_HEADS, 1),
                                 jnp.where(lane < 3 * N_HEADS, pltpu.roll(t2, 2 * N_HEADS, 1), 0.0)))
    return packed.astype(_BF16)


def _proj_fox_kernel(x_ref, g_ref, w_ref, wf_ref, bf_ref, tri_ref, sel_ref, o_ref, ka_ref,
                     h_ref, carry_ref, *, n_q_blocks, blocks_per_seq):
    i = pl.program_id(0)
    j = pl.program_id(1)

    @pl.when(j == 0)
    def _():
        x = x_ref[...]
        hf = x * _rms_scale(x) * g_ref[...]
        h_hi = hf.astype(_BF16)
        h_ref[...] = h_hi
        h_lo = (hf - h_hi.astype(_F32)).astype(_BF16)
        both = lax.dot_general(h_hi, wf_ref[...], _NT, preferred_element_type=_F32)
        logit = (both[:, :LANES] + both[:, LANES:]
                 + lax.dot_general(h_lo, wf_ref[:LANES, :], _NT,
                                   preferred_element_type=_F32))
        z = logit + bf_ref[...]
        log_f = jnp.minimum(z, 0.0) - jnp.log1p(jnp.exp(-jnp.abs(z)))
        lane = lax.broadcasted_iota(jnp.int32, log_f.shape, 1)
        part = jnp.dot(tri_ref[...], _pack_terms(log_f, lane), preferred_element_type=_F32)
        csum = (part + pltpu.roll(part, LANES - N_HEADS, 1)
                + pltpu.roll(part, LANES - 2 * N_HEADS, 1))

        @pl.when(i % blocks_per_seq == 0)
        def _():
            carry_ref[...] = jnp.zeros_like(carry_ref)

        c = csum + carry_ref[...]
        carry_ref[...] = c[PROJ_TM - 1:PROJ_TM, :]
        ka = jnp.dot(_pack_terms(c * -LOG2E, lane), sel_ref[...], preferred_element_type=_F32)
        ka_ref[...] = ka.astype(_BF16)

    res = lax.dot_general(h_ref[...], w_ref[...], _NT, preferred_element_type=_F32)
    scale = jnp.where(j < n_q_blocks, HEAD_DIM ** -0.5 * LOG2E, 1.0).astype(_F32)
    o_ref[...] = (res * scale).astype(_BF16)


def _bias_lane_selector():
    n_pairs = N_HEADS // HEADS_PER_TILE
    row = jnp.arange(LANES)[:, None]
    s, h = row // N_HEADS, row % N_HEADS
    col = jnp.arange(n_pairs * LANES)[None, :]
    target = (h // HEADS_PER_TILE) * LANES + BIAS_TERMS * (h % HEADS_PER_TILE) + s
    return ((col == target) & (s < BIAS_TERMS)).astype(_BF16)


def _proj_fox(x2, gain, w_all, idx, wf_hi_lo, bf_pad, seq_len):
    t, d = x2.shape
    n = 3 * N_HEADS * HEAD_DIM
    ka_cols = (N_HEADS // HEADS_PER_TILE) * LANES
    tri = (lax.broadcasted_iota(jnp.int32, (PROJ_TM, PROJ_TM), 1)
           <= lax.broadcasted_iota(jnp.int32, (PROJ_TM, PROJ_TM), 0)).astype(_BF16)
    kern = functools.partial(_proj_fox_kernel,
                             n_q_blocks=(N_HEADS * HEAD_DIM) // PROJ_TN,
                             blocks_per_seq=seq_len // PROJ_TM)
    return pl.pallas_call(
        kern,
        grid=(t // PROJ_TM, n // PROJ_TN),
        in_specs=[
            pl.BlockSpec((PROJ_TM, d), lambda i, j: (i, 0)),
            pl.BlockSpec((1, d), lambda i, j: (0, 0)),
            pl.BlockSpec((None, PROJ_TN, d), lambda i, j: (idx, j, 0)),
            pl.BlockSpec((2 * LANES, d), lambda i, j: (0, 0)),
            pl.BlockSpec((1, LANES), lambda i, j: (0, 0)),
            pl.BlockSpec((PROJ_TM, PROJ_TM), lambda i, j: (0, 0)),
            pl.BlockSpec((LANES, ka_cols), lambda i, j: (0, 0)),
        ],
        out_specs=[
            pl.BlockSpec((PROJ_TM, PROJ_TN), lambda i, j: (i, j)),
            pl.BlockSpec((PROJ_TM, ka_cols), lambda i, j: (i, 0)),
        ],
        out_shape=[jax.ShapeDtypeStruct((t, n), _BF16),
                   jax.ShapeDtypeStruct((t, ka_cols), _BF16)],
        scratch_shapes=[pltpu.VMEM((PROJ_TM, d), _BF16),
                        pltpu.VMEM((1, LANES), _F32)],
        compiler_params=_params(("arbitrary", "arbitrary")),
        name="proj_fox",
    )(x2, gain, w_all, wf_hi_lo, bf_pad, tri, _bias_lane_selector())


def _swa_kernel(sink_ref, q_ref, kp_ref, kc_ref, vp_ref, vc_ref, o_ref):
    g = pl.program_id(1)
    n = pl.program_id(2)
    lane = lax.broadcasted_iota(jnp.int32, (1, LANES), 1)
    low = lane < HEAD_DIM

    def both_halves(prev_ref, cur_ref, tile):
        cols = slice(tile * LANES, (tile + 1) * LANES)
        band = jnp.concatenate([prev_ref[:, cols], cur_ref[:, cols]], axis=0).astype(_F32)
        swapped = pltpu.roll(band, HEAD_DIM, 1)
        return (jnp.where(low, band, swapped).astype(_BF16),
                jnp.where(low, swapped, band).astype(_BF16))

    q_pos = lax.broadcasted_iota(jnp.int32, (BLOCK, 2 * BLOCK), 0)
    s_pos = lax.broadcasted_iota(jnp.int32, (BLOCK, 2 * BLOCK), 1)
    rel = BLOCK + q_pos - s_pos
    valid = (rel >= 0) & (rel < WINDOW) & ((s_pos >= BLOCK) | (n > 0))

    zero = jnp.zeros((), _BF16)
    chunks_per_kv = SWA_GROUP // HEADS_PER_TILE
    dups = [(both_halves(kp_ref, kc_ref, tile), both_halves(vp_ref, vc_ref, tile))
            for tile in range(SWA_TILES)]
    for tile, kv in [(tile, kv) for tile in range(SWA_TILES) for kv in range(HEADS_PER_TILE)]:
        k_dup, v_dup = dups[tile]
        kv_head = tile * HEADS_PER_TILE + kv
        chunks = [q_ref[:, (chunks_per_kv * kv_head + c) * LANES:(chunks_per_kv * kv_head + c + 1) * LANES]
                  for c in range(chunks_per_kv)]
        q_stack = jnp.concatenate(
            [jnp.where(low if hh == 0 else ~low, qc, zero)
             for qc in chunks for hh in range(HEADS_PER_TILE)], axis=0)
        s_all = lax.dot_general(q_stack, k_dup[kv], _NT, preferred_element_type=_F32)
        probs, sink_terms = [], []
        for h in range(SWA_GROUP):
            s = jnp.where(valid, s_all[h * BLOCK:(h + 1) * BLOCK], MASK_VALUE)
            sink = sink_ref[(g * SWA_TILES * HEADS_PER_TILE + kv_head) * SWA_GROUP + h] * LOG2E
            m = jnp.maximum(jnp.max(s, axis=-1, keepdims=True), sink)
            probs.append(jnp.exp2(s - m).astype(_BF16))
            sink_terms.append(jnp.exp2(sink - m))
        v_ones = jnp.concatenate([v_dup[kv], jnp.ones((2 * BLOCK, LANES), _BF16)], axis=1)
        o_all = jnp.dot(jnp.concatenate(probs, axis=0), v_ones,
                        preferred_element_type=_F32)

        def normalised(h):
            rows = slice(h * BLOCK, (h + 1) * BLOCK)
            return o_all[rows, :LANES] / (o_all[rows, LANES:] + sink_terms[h])

        for c in range(chunks_per_kv):
            o0 = normalised(HEADS_PER_TILE * c)
            o1 = normalised(HEADS_PER_TILE * c + 1)
            col = (chunks_per_kv * kv_head + c) * LANES
            o_ref[:, col:col + LANES] = jnp.where(low, o0, o1).astype(_BF16)


def _swa_attention(qkv, sinks, batch, seq_len):
    t = qkv.shape[0]
    nblk = seq_len // BLOCK
    qd = N_HEADS * HEAD_DIM
    kd = SWA_KV_HEADS * HEAD_DIM
    kv_tile = SWA_TILES * LANES
    q_tile = kv_tile * SWA_GROUP
    k_col0 = qd // kv_tile
    v_col0 = (qd + kd) // kv_tile

    def cur(col0):
        return lambda b, g, n: (b * nblk + n, col0 + g)

    def prev(col0):
        return lambda b, g, n: (b * nblk + jnp.maximum(n - 1, 0), col0 + g)

    return pl.pallas_call(
        _swa_kernel,
        grid=(batch, kd // kv_tile, nblk),
        in_specs=[
            pl.BlockSpec(memory_space=pltpu.SMEM),
            pl.BlockSpec((BLOCK, q_tile), lambda b, g, n: (b * nblk + n, g)),
            pl.BlockSpec((BLOCK, kv_tile), prev(k_col0)),
            pl.BlockSpec((BLOCK, kv_tile), cur(k_col0)),
            pl.BlockSpec((BLOCK, kv_tile), prev(v_col0)),
            pl.BlockSpec((BLOCK, kv_tile), cur(v_col0)),
        ],
        out_specs=pl.BlockSpec((BLOCK, q_tile), lambda b, g, n: (b * nblk + n, g)),
        out_shape=jax.ShapeDtypeStruct((t, qd), _BF16),
        compiler_params=_params(("arbitrary", "arbitrary", "arbitrary")),
        name="swa_attn",
    )(sinks, qkv, qkv, qkv, qkv, qkv)


def _fox_kernel(q_ref, k_ref, ka_ref, v_ref, o_ref, qa_ref, s_ref, m_ref, acc_ref):
    i = pl.program_id(2)
    t = FOX_T
    lane = lax.broadcasted_iota(jnp.int32, (1, LANES), 1)
    low = lane < HEAD_DIM
    zero = jnp.zeros((), _BF16)
    heads = [(pr, hd) for pr in range(FOX_PAIRS) for hd in range(HEADS_PER_TILE)]

    def lanes_of(pr):
        return slice(pr * LANES, (pr + 1) * LANES)

    for h, (pr, hd) in enumerate(heads):
        own = low if hd == 0 else ~low
        bias_lanes = (lane >= BIAS_TERMS * hd) & (lane < BIAS_TERMS * (hd + 1))
        pick = jnp.broadcast_to(jnp.where(bias_lanes, 1.0, 0.0).astype(_BF16), (t, LANES))
        qa_ref[h] = jnp.concatenate([jnp.where(own, q_ref[:, lanes_of(pr)], zero), pick], axis=1)
        m_ref[h] = jnp.full((t, LANES), MASK_VALUE, _F32)
        acc_ref[h] = jnp.zeros((t, 2 * LANES), _F32)

    def scores(kb, slot):
        ks = pl.multiple_of(kb * t, t)
        for pr in range(FOX_PAIRS):
            kt = jnp.concatenate([k_ref[pl.ds(ks, t), lanes_of(pr)],
                                  ka_ref[pl.ds(ks, t), lanes_of(pr)]], axis=1)
            for hd in range(HEADS_PER_TILE):
                h = HEADS_PER_TILE * pr + hd
                s_ref[slot, h] = lax.dot_general(qa_ref[h], kt, _NT, preferred_element_type=_F32)

    def absorb(kb, slot):
        ks = pl.multiple_of(kb * t, t)
        ones = jnp.ones((t, LANES), _BF16)
        for h, (pr, hd) in enumerate(heads):
            vt = jnp.concatenate([v_ref[pl.ds(ks, t), lanes_of(pr)], ones], axis=1)
            s = s_ref[slot, h]
            m = m_ref[h]
            m_new = jnp.maximum(m, jnp.max(s, axis=-1, keepdims=True))
            p = jnp.exp2(s - jnp.concatenate([m_new] * (t // LANES), axis=1)).astype(_BF16)
            alpha = jnp.exp2(m - m_new)
            acc_ref[h] = (jnp.concatenate([alpha, alpha], axis=1) * acc_ref[h]
                          + jnp.dot(p, vt, preferred_element_type=_F32))
            m_ref[h] = m_new

    def absorb_diagonal(kb, slot):
        ks = pl.multiple_of(kb * t, t)
        ones = jnp.ones((t, LANES), _BF16)
        for h, (pr, hd) in enumerate(heads):
            vt = jnp.concatenate([v_ref[pl.ds(ks, t), lanes_of(pr)], ones], axis=1)
            for r0, r1 in ((0, t // 2), (t // 2, t)):
                causal = (lax.broadcasted_iota(jnp.int32, (r1 - r0, r1), 1)
                          <= r0 + lax.broadcasted_iota(jnp.int32, (r1 - r0, r1), 0))
                s = jnp.where(causal, s_ref[slot, h, r0:r1, :r1], MASK_VALUE)
                m = m_ref[h, r0:r1, :]
                m_new = jnp.maximum(m, jnp.max(s, axis=-1, keepdims=True))
                p = jnp.exp2(s - jnp.concatenate([m_new] * (r1 // LANES), axis=1)).astype(_BF16)
                alpha = jnp.exp2(m - m_new)
                acc_ref[h, r0:r1, :] = (jnp.concatenate([alpha, alpha], axis=1) * acc_ref[h, r0:r1, :]
                                        + jnp.dot(p, vt[:r1], preferred_element_type=_F32))
                m_ref[h, r0:r1, :] = m_new

    scores(0, 0)

    def two_blocks(kb):
        scores(kb + 1, 1)
        absorb(kb, 0)
        scores(kb + 2, 0)
        absorb(kb + 1, 1)

    def four_blocks(j, carry):
        two_blocks(4 * j)
        two_blocks(4 * j + 2)
        return carry

    lax.fori_loop(0, lax.shift_right_logical(i, 2), four_blocks, 0)

    @pl.when((i & 2) != 0)
    def _():
        two_blocks(i & ~3)

    @pl.when((i & 1) == 1)
    def _():
        scores(i, 1)
        absorb(i - 1, 0)
        absorb_diagonal(i, 1)

    @pl.when((i & 1) == 0)
    def _():
        absorb_diagonal(i, 0)

    for pr in range(FOX_PAIRS):
        outs = [acc_ref[HEADS_PER_TILE * pr + hd][:, :LANES] / acc_ref[HEADS_PER_TILE * pr + hd][:, LANES:]
                for hd in range(HEADS_PER_TILE)]
        o_ref[:, lanes_of(pr)] = jnp.where(low, outs[0], outs[1]).astype(_BF16)


def _fox_attention(qkv, ka, batch, seq_len):
    t = qkv.shape[0]
    hd = N_HEADS * HEAD_DIM
    n_steps = N_HEADS // (HEADS_PER_TILE * FOX_PAIRS)
    width = FOX_PAIRS * LANES
    heads = FOX_PAIRS * HEADS_PER_TILE
    nq = seq_len // FOX_T
    return pl.pallas_call(
        _fox_kernel,
        grid=(batch, n_steps, nq),
        in_specs=[
            pl.BlockSpec((FOX_T, width), lambda b, p, i: (b * nq + i, p)),
            pl.BlockSpec((seq_len, width), lambda b, p, i: (b, n_steps + p)),
            pl.BlockSpec((seq_len, width), lambda b, p, i: (b, p)),
            pl.BlockSpec((seq_len, width), lambda b, p, i: (b, 2 * n_steps + p)),
        ],
        out_specs=pl.BlockSpec((FOX_T, width), lambda b, p, i: (b * nq + i, p)),
        out_shape=jax.ShapeDtypeStruct((t, hd), _BF16),
        scratch_shapes=[
            pltpu.VMEM((heads, FOX_T, 2 * LANES), _BF16),
            pltpu.VMEM((2, heads, FOX_T, FOX_T), _F32),
            pltpu.VMEM((heads, FOX_T, LANES), _F32),
            pltpu.VMEM((heads, FOX_T, 2 * LANES), _F32),
        ],
        compiler_params=_params(("arbitrary", "arbitrary", "arbitrary")),
        name="fox_attn",
    )(qkv, qkv, ka, qkv)


def _out_kernel(a_ref, w_ref, x_ref, g1_ref, g2_ref, xo_ref, h_ref):
    half = a_ref.shape[0] // 2
    for r in (slice(0, half), slice(half, 2 * half)):
        y = jnp.dot(a_ref[r, :], w_ref[...], preferred_element_type=_F32)
        xn = x_ref[r, :] + y * _rms_scale(y) * g1_ref[...]
        xo_ref[r, :] = xn
        h_ref[r, :] = (xn * _rms_scale(xn) * g2_ref[...]).astype(_BF16)


def _out_proj(attn, w_all, idx, x2, g_post, g_pre_ffn):
    t, d = x2.shape
    k = attn.shape[1]
    row = lambda i: (i, 0)
    fixed = lambda i: (0, 0)
    return pl.pallas_call(
        _out_kernel,
        grid=(t // OUT_TM,),
        in_specs=[
            pl.BlockSpec((OUT_TM, k), row),
            pl.BlockSpec((None, k, d), lambda i: (idx, 0, 0)),
            pl.BlockSpec((OUT_TM, d), row),
            pl.BlockSpec((1, d), fixed),
            pl.BlockSpec((1, d), fixed),
        ],
        out_specs=[pl.BlockSpec((OUT_TM, d), row), pl.BlockSpec((OUT_TM, d), row)],
        out_shape=[jax.ShapeDtypeStruct((t, d), _F32), jax.ShapeDtypeStruct((t, d), _BF16)],
        compiler_params=_params(("arbitrary",)),
        name="out_proj",
    )(attn, w_all, x2, g_post, g_pre_ffn)


def _ffn_kernel(h_ref, wg_ref, wu_ref, wd_ref, x_ref, g_ref, o_ref, acc_ref, *, nf, n_rows):
    s = pl.program_id(0)
    i = s // nf
    f = s - i * nf

    def chunk():
        h = h_ref[...]
        gate = jnp.dot(h, wg_ref[...], preferred_element_type=_F32)
        up = jnp.dot(h, wu_ref[...], preferred_element_type=_F32)
        act = (gate * (1.0 / (1.0 + jnp.exp(-gate))) * up).astype(_BF16)
        return jnp.dot(act, wd_ref[...], preferred_element_type=_F32)

    def finish_previous():
        y = acc_ref[...]
        o_ref[...] = x_ref[...] + y * _rms_scale(y) * g_ref[...]

    @pl.when(f > 0)
    def _():
        acc_ref[...] += chunk()

    @pl.when((f == 0) & (i == 0))
    def _():
        acc_ref[...] = chunk()

    @pl.when((f == 0) & (i > 0) & (i < n_rows))
    def _():
        finish_previous()
        acc_ref[...] = chunk()

    @pl.when(i == n_rows)
    def _():
        finish_previous()


def _ffn(h, w_gu_all, w_down_all, idx, x2, g_post):
    t, d = x2.shape
    d_ff = w_down_all.shape[1]
    nf = d_ff // FFN_TF
    n_rows = t // FFN_TM
    last = n_rows * nf

    def row(s):
        return jnp.minimum(s // nf, n_rows - 1)

    def col(s):
        return jnp.where(s < last, s % nf, nf - 1)

    def prev_row(s):
        return jnp.maximum(s // nf - 1, 0)

    kern = functools.partial(_ffn_kernel, nf=nf, n_rows=n_rows)
    return pl.pallas_call(
        kern,
        grid=(last + 1,),
        in_specs=[
            pl.BlockSpec((FFN_TM, d), lambda s: (row(s), 0)),
            pl.BlockSpec((None, d, FFN_TF), lambda s: (idx, 0, col(s))),
            pl.BlockSpec((None, d, FFN_TF), lambda s: (idx, 0, nf + col(s))),
            pl.BlockSpec((None, FFN_TF, d), lambda s: (idx, col(s), 0)),
            pl.BlockSpec((FFN_TM, d), lambda s: (prev_row(s), 0)),
            pl.BlockSpec((1, d), lambda s: (0, 0)),
        ],
        out_specs=pl.BlockSpec((FFN_TM, d), lambda s: (prev_row(s), 0)),
        out_shape=jax.ShapeDtypeStruct((t, d), _F32),
        scratch_shapes=[pltpu.VMEM((FFN_TM, d), _F32)],
        compiler_params=_params(("arbitrary",)),
        name="ffn",
    )(h, w_gu_all, w_gu_all, w_down_all, x2, g_post)


def _rotary_lane_tables(positions):
    half = ROT_DIM // 2
    inv_freq = ROPE_THETA ** (-jnp.arange(0, ROT_DIM, 2, dtype=_F32) / ROT_DIM)
    ang = positions.astype(_F32).reshape(-1, 1) * inv_freq
    cos, sin = jnp.cos(ang), jnp.sin(ang)
    t = ang.shape[0]
    pad = HEAD_DIM - ROT_DIM
    ra = jnp.concatenate([cos, cos, jnp.ones((t, pad), _F32)], axis=1)
    rb = jnp.concatenate([-sin, jnp.zeros((t, half + pad), _F32)], axis=1)
    rc = jnp.concatenate([jnp.zeros((t, half), _F32), sin, jnp.zeros((t, pad), _F32)], axis=1)
    tile = lambda a: jnp.tile(a, (1, HEADS_PER_TILE))
    return tile(ra), tile(rb), tile(rc)


def kernel(x, positions, norm_gains, swa_w_in, swa_sinks, swa_w_out, fox_w_in, fox_b_f,
           fox_w_out, ffn_w_gate_up, ffn_w_down):
    batch, seq_len, d = x.shape
    depth = norm_gains.shape[0]
    hd = N_HEADS * HEAD_DIM
    assert d == D_MODEL and seq_len % PROJ_TM == 0 and seq_len % FOX_T == 0
    assert WINDOW <= BLOCK

    x2 = x.reshape(batch * seq_len, d)
    ra, rb, rc = _rotary_lane_tables(positions)
    gains = norm_gains.reshape(depth, 4, 1, d)
    swa_w_in_b, swa_w_out_b = swa_w_in.astype(_BF16), swa_w_out.astype(_BF16)
    fox_w_in_t = jnp.swapaxes(fox_w_in, 1, 2)
    fox_w_in_b, fox_w_out_b = fox_w_in_t.astype(_BF16), fox_w_out.astype(_BF16)
    ffn_w_gu_b, ffn_w_down_b = ffn_w_gate_up.astype(_BF16), ffn_w_down.astype(_BF16)

    for layer in range(depth):
        j = layer // 2
        g = gains[layer]
        if layer % 2 == 0:
            qkv = _proj_swa(x2, g[0], swa_w_in_b, j, ra, rb, rc)
            attn = _swa_attention(qkv, swa_sinks[j].astype(_F32), batch, seq_len)
            w_out = swa_w_out_b
        else:
            wf_pad = jnp.pad(fox_w_in_t[j, 3 * hd:, :].astype(_F32), ((0, LANES - N_HEADS), (0, 0)))
            wf_hi = wf_pad.astype(_BF16)
            wf_lo = (wf_pad - wf_hi.astype(_F32)).astype(_BF16)
            bf_pad = jnp.pad(fox_b_f[j].astype(_F32), (0, LANES - N_HEADS)).reshape(1, LANES)
            qkv, ka = _proj_fox(x2, g[0], fox_w_in_b, j,
                                jnp.concatenate([wf_hi, wf_lo], axis=0), bf_pad, seq_len)
            attn = _fox_attention(qkv, ka, batch, seq_len)
            w_out = fox_w_out_b
        x2, h = _out_proj(attn, w_out, j, x2, g[1], g[2])
        x2 = _ffn(h, ffn_w_gu_b, ffn_w_down_b, layer, x2, g[3])
    return x2.reshape(batch, seq_len, d)
```

```python
import functools

import jax
import jax.numpy as jnp
from jax import lax
from jax.experimental import pallas as pl
from jax.experimental.pallas import tpu as pltpu

D_MODEL = 2048
HEAD_DIM = 64
N_HEADS = D_MODEL // HEAD_DIM
SWA_KV_HEADS = 8
SWA_GROUP = N_HEADS // SWA_KV_HEADS
WINDOW = 128
BLOCK = 128
ROPE_THETA = 500000.0
ROT_DIM = HEAD_DIM // 4
RMS_EPS = 1e-6

LANES = 128
HEADS_PER_TILE = LANES // HEAD_DIM
VMEM_LIMIT_BYTES = 56 * 1024 * 1024
MASK_VALUE = -1e30

PROJ_TM = 1024
PROJ_TN = 1024
SWA_PROJ_TN = 1024
OUT_TM = 512
FFN_TM = 512
FFN_TF = 512
FOX_T = 512
SWA_TILES = 4
FOX_PAIRS = 2
BIAS_TERMS = 3
LOG2E = 1.4426950408889634

_F32 = jnp.float32
_BF16 = jnp.bfloat16
_NT = (((1,), (1,)), ((), ()))


def _params(semantics):
    return pltpu.CompilerParams(dimension_semantics=semantics,
                                vmem_limit_bytes=VMEM_LIMIT_BYTES)


def _rms_scale(v):
    return lax.rsqrt(jnp.mean(v * v, axis=-1, keepdims=True) + RMS_EPS)


def _proj_swa_kernel(x_ref, g_ref, w_ref, ra_ref, rb_ref, rc_ref, o_ref, h_ref,
                     *, n_q_tiles, n_rope_tiles):
    j = pl.program_id(1)

    @pl.when(j == 0)
    def _():
        x = x_ref[...]
        h_ref[...] = (x * _rms_scale(x) * g_ref[...]).astype(_BF16)

    res = jnp.dot(h_ref[...], w_ref[...], preferred_element_type=_F32)

    half = ROT_DIM // 2
    tiles = res.shape[1] // LANES
    for c in range(tiles):
        tile = j * tiles + c
        rope = (tile < n_rope_tiles).astype(_F32)
        scale = jnp.where(tile < n_q_tiles, HEAD_DIM ** -0.5 * LOG2E, 1.0).astype(_F32)
        ra = (ra_ref[...] * rope + (1.0 - rope)) * scale
        rb = rb_ref[...] * (rope * scale)
        rc = rc_ref[...] * (rope * scale)
        r = res[:, c * LANES:(c + 1) * LANES]
        rot = r * ra + pltpu.roll(r, LANES - half, 1) * rb + pltpu.roll(r, half, 1) * rc
        o_ref[:, c * LANES:(c + 1) * LANES] = rot.astype(_BF16)


def _proj_swa(x2, gain, w_all, idx, ra, rb, rc):
    t, d = x2.shape
    n = w_all.shape[2]
    qd = N_HEADS * HEAD_DIM
    kd = SWA_KV_HEADS * HEAD_DIM
    kern = functools.partial(_proj_swa_kernel, n_q_tiles=qd // LANES,
                             n_rope_tiles=(qd + kd) // LANES)
    return pl.pallas_call(
        kern,
        grid=(t // PROJ_TM, n // SWA_PROJ_TN),
        in_specs=[
            pl.BlockSpec((PROJ_TM, d), lambda i, j: (i, 0)),
            pl.BlockSpec((1, d), lambda i, j: (0, 0)),
            pl.BlockSpec((None, d, SWA_PROJ_TN), lambda i, j: (idx, 0, j)),
            pl.BlockSpec((PROJ_TM, LANES), lambda i, j: (i, 0)),
            pl.BlockSpec((PROJ_TM, LANES), lambda i, j: (i, 0)),
            pl.BlockSpec((PROJ_TM, LANES), lambda i, j: (i, 0)),
        ],
        out_specs=pl.BlockSpec((PROJ_TM, SWA_PROJ_TN), lambda i, j: (i, j)),
        out_shape=jax.ShapeDtypeStruct((t, n), _BF16),
        scratch_shapes=[pltpu.VMEM((PROJ_TM, d), _BF16)],
        compiler_params=_params(("arbitrary", "arbitrary")),
        name="proj_swa",
    )(x2, gain, w_all, ra, rb, rc)


def _pack_terms(v, lane):
    t0 = v.astype(_BF16).astype(_F32)
    r1 = v - t0
    t1 = r1.astype(_BF16).astype(_F32)
    t2 = (r1 - t1).astype(_BF16).astype(_F32)
    packed = jnp.where(lane < N_HEADS, t0,
                       jnp.where(lane < 2 * N_HEADS, pltpu.roll(t1, N_HEADS, 1),
                                 jnp.where(lane < 3 * N_HEADS, pltpu.roll(t2, 2 * N_HEADS, 1), 0.0)))
    return packed.astype(_BF16)


def _proj_fox_kernel(x_ref, g_ref, w_ref, wf_ref, bf_ref, tri_ref, sel_ref, o_ref, ka_ref,
                     h_ref, carry_ref, *, n_q_blocks, blocks_per_seq):
    i = pl.program_id(0)
    j = pl.program_id(1)

    @pl.when(j == 0)
    def _():
        x = x_ref[...]
        hf = x * _rms_scale(x) * g_ref[...]
        h_hi = hf.astype(_BF16)
        h_ref[...] = h_hi
        h_lo = (hf - h_hi.astype(_F32)).astype(_BF16)
        both = lax.dot_general(h_hi, wf_ref[...], _NT, preferred_element_type=_F32)
        logit = (both[:, :LANES] + both[:, LANES:]
                 + lax.dot_general(h_lo, wf_ref[:LANES, :], _NT,
                                   preferred_element_type=_F32))
        z = logit + bf_ref[...]
        log_f = jnp.minimum(z, 0.0) - jnp.log1p(jnp.exp(-jnp.abs(z)))
        lane = lax.broadcasted_iota(jnp.int32, log_f.shape, 1)
        part = jnp.dot(tri_ref[...], _pack_terms(log_f, lane), preferred_element_type=_F32)
        csum = (part + pltpu.roll(part, LANES - N_HEADS, 1)
                + pltpu.roll(part, LANES - 2 * N_HEADS, 1))

        @pl.when(i % blocks_per_seq == 0)
        def _():
            carry_ref[...] = jnp.zeros_like(carry_ref)

        c = csum + carry_ref[...]
        carry_ref[...] = c[PROJ_TM - 1:PROJ_TM, :]
        ka = jnp.dot(_pack_terms(c * -LOG2E, lane), sel_ref[...], preferred_element_type=_F32)
        ka_ref[...] = ka.astype(_BF16)

    res = lax.dot_general(h_ref[...], w_ref[...], _NT, preferred_element_type=_F32)
    scale = jnp.where(j < n_q_blocks, HEAD_DIM ** -0.5 * LOG2E, 1.0).astype(_F32)
    o_ref[...] = (res * scale).astype(_BF16)


def _bias_lane_selector():
    n_pairs = N_HEADS // HEADS_PER_TILE
    row = jnp.arange(LANES)[:, None]
    s, h = row // N_HEADS, row % N_HEADS
    col = jnp.arange(n_pairs * LANES)[None, :]
    target = (h // HEADS_PER_TILE) * LANES + BIAS_TERMS * (h % HEADS_PER_TILE) + s
    return ((col == target) & (s < BIAS_TERMS)).astype(_BF16)


def _proj_fox(x2, gain, w_all, idx, wf_hi_lo, bf_pad, seq_len):
    t, d = x2.shape
    n = 3 * N_HEADS * HEAD_DIM
    ka_cols = (N_HEADS // HEADS_PER_TILE) * LANES
    tri = (lax.broadcasted_iota(jnp.int32, (PROJ_TM, PROJ_TM), 1)
           <= lax.broadcasted_iota(jnp.int32, (PROJ_TM, PROJ_TM), 0)).astype(_BF16)
    kern = functools.partial(_proj_fox_kernel,
                             n_q_blocks=(N_HEADS * HEAD_DIM) // PROJ_TN,
                             blocks_per_seq=seq_len // PROJ_TM)
    return pl.pallas_call(
        kern,
        grid=(t // PROJ_TM, n // PROJ_TN),
        in_specs=[
            pl.BlockSpec((PROJ_TM, d), lambda i, j: (i, 0)),
            pl.BlockSpec((1, d), lambda i, j: (0, 0)),
            pl.BlockSpec((None, PROJ_TN, d), lambda i, j: (idx, j, 0)),
            pl.BlockSpec((2 * LANES, d), lambda i, j: (0, 0)),
            pl.BlockSpec((1, LANES), lambda i, j: (0, 0)),
            pl.BlockSpec((PROJ_TM, PROJ_TM), lambda i, j: (0, 0)),
            pl.BlockSpec((LANES, ka_cols), lambda i, j: (0, 0)),
        ],
        out_specs=[
            pl.BlockSpec((PROJ_TM, PROJ_TN), lambda i, j: (i, j)),
            pl.BlockSpec((PROJ_TM, ka_cols), lambda i, j: (i, 0)),
        ],
        out_shape=[jax.ShapeDtypeStruct((t, n), _BF16),
                   jax.ShapeDtypeStruct((t, ka_cols), _BF16)],
        scratch_shapes=[pltpu.VMEM((PROJ_TM, d), _BF16),
                        pltpu.VMEM((1, LANES), _F32)],
        compiler_params=_params(("arbitrary", "arbitrary")),
        name="proj_fox",
    )(x2, gain, w_all, wf_hi_lo, bf_pad, tri, _bias_lane_selector())


def _swa_kernel(sink_ref, q_ref, kp_ref, kc_ref, vp_ref, vc_ref, o_ref):
    g = pl.program_id(1)
    n = pl.program_id(2)
    lane = lax.broadcasted_iota(jnp.int32, (1, LANES), 1)
    low = lane < HEAD_DIM

    def both_halves(prev_ref, cur_ref, tile):
        cols = slice(tile * LANES, (tile + 1) * LANES)
        band = jnp.concatenate([prev_ref[:, cols], cur_ref[:, cols]], axis=0).astype(_F32)
        swapped = pltpu.roll(band, HEAD_DIM, 1)
        return (jnp.where(low, band, swapped).astype(_BF16),
                jnp.where(low, swapped, band).astype(_BF16))

    q_pos = lax.broadcasted_iota(jnp.int32, (BLOCK, 2 * BLOCK), 0)
    s_pos = lax.broadcasted_iota(jnp.int32, (BLOCK, 2 * BLOCK), 1)
    rel = BLOCK + q_pos - s_pos
    valid = (rel >= 0) & (rel < WINDOW) & ((s_pos >= BLOCK) | (n > 0))

    zero = jnp.zeros((), _BF16)
    chunks_per_kv = SWA_GROUP // HEADS_PER_TILE
    dups = [(both_halves(kp_ref, kc_ref, tile), both_halves(vp_ref, vc_ref, tile))
            for tile in range(SWA_TILES)]
    for tile, kv in [(tile, kv) for tile in range(SWA_TILES) for kv in range(HEADS_PER_TILE)]:
        k_dup, v_dup = dups[tile]
        kv_head = tile * HEADS_PER_TILE + kv
        chunks = [q_ref[:, (chunks_per_kv * kv_head + c) * LANES:(chunks_per_kv * kv_head + c + 1) * LANES]
                  for c in range(chunks_per_kv)]
        q_stack = jnp.concatenate(
            [jnp.where(low if hh == 0 else ~low, qc, zero)
             for qc in chunks for hh in range(HEADS_PER_TILE)], axis=0)
        s_all = lax.dot_general(q_stack, k_dup[kv], _NT, preferred_element_type=_F32)
        probs, sink_terms = [], []
        for h in range(SWA_GROUP):
            s = jnp.where(valid, s_all[h * BLOCK:(h + 1) * BLOCK], MASK_VALUE)
            sink = sink_ref[(g * SWA_TILES * HEADS_PER_TILE + kv_head) * SWA_GROUP + h] * LOG2E
            m = jnp.maximum(jnp.max(s, axis=-1, keepdims=True), sink)
            probs.append(jnp.exp2(s - m).astype(_BF16))
            sink_terms.append(jnp.exp2(sink - m))
        v_ones = jnp.concatenate([v_dup[kv], jnp.ones((2 * BLOCK, LANES), _BF16)], axis=1)
        o_all = jnp.dot(jnp.concatenate(probs, axis=0), v_ones,
                        preferred_element_type=_F32)

        def normalised(h):
            rows = slice(h * BLOCK, (h + 1) * BLOCK)
            return o_all[rows, :LANES] / (o_all[rows, LANES:] + sink_terms[h])

        for c in range(chunks_per_kv):
            o0 = normalised(HEADS_PER_TILE * c)
            o1 = normalised(HEADS_PER_TILE * c + 1)
            col = (chunks_per_kv * kv_head + c) * LANES
            o_ref[:, col:col + LANES] = jnp.where(low, o0, o1).astype(_BF16)


def _swa_attention(qkv, sinks, batch, seq_len):
    t = qkv.shape[0]
    nblk = seq_len // BLOCK
    qd = N_HEADS * HEAD_DIM
    kd = SWA_KV_HEADS * HEAD_DIM
    kv_tile = SWA_TILES * LANES
    q_tile = kv_tile * SWA_GROUP
    k_col0 = qd // kv_tile
    v_col0 = (qd + kd) // kv_tile

    def cur(col0):
        return lambda b, g, n: (b * nblk + n, col0 + g)

    def prev(col0):
        return lambda b, g, n: (b * nblk + jnp.maximum(n - 1, 0), col0 + g)

    return pl.pallas_call(
        _swa_kernel,
        grid=(batch, kd // kv_tile, nblk),
        in_specs=[
            pl.BlockSpec(memory_space=pltpu.SMEM),
            pl.BlockSpec((BLOCK, q_tile), lambda b, g, n: (b * nblk + n, g)),
            pl.BlockSpec((BLOCK, kv_tile), prev(k_col0)),
            pl.BlockSpec((BLOCK, kv_tile), cur(k_col0)),
            pl.BlockSpec((BLOCK, kv_tile), prev(v_col0)),
            pl.BlockSpec((BLOCK, kv_tile), cur(v_col0)),
        ],
        out_specs=pl.BlockSpec((BLOCK, q_tile), lambda b, g, n: (b * nblk + n, g)),
        out_shape=jax.ShapeDtypeStruct((t, qd), _BF16),
        compiler_params=_params(("arbitrary", "arbitrary", "arbitrary")),
        name="swa_attn",
    )(sinks, qkv, qkv, qkv, qkv, qkv)


def _fox_kernel(q_ref, k_ref, ka_ref, v_ref, o_ref, qa_ref, s_ref, m_ref, acc_ref):
    i = pl.program_id(2)
    t = FOX_T
    lane = lax.broadcasted_iota(jnp.int32, (1, LANES), 1)
    low = lane < HEAD_DIM
    zero = jnp.zeros((), _BF16)
    heads = [(pr, hd) for pr in range(FOX_PAIRS) for hd in range(HEADS_PER_TILE)]

    def lanes_of(pr):
        return slice(pr * LANES, (pr + 1) * LANES)

    for h, (pr, hd) in enumerate(heads):
        own = low if hd == 0 else ~low
        bias_lanes = (lane >= BIAS_TERMS * hd) & (lane < BIAS_TERMS * (hd + 1))
        pick = jnp.broadcast_to(jnp.where(bias_lanes, 1.0, 0.0).astype(_BF16), (t, LANES))
        qa_ref[h] = jnp.concatenate([jnp.where(own, q_ref[:, lanes_of(pr)], zero), pick], axis=1)
        m_ref[h] = jnp.full((t, LANES), MASK_VALUE, _F32)
        acc_ref[h] = jnp.zeros((t, 2 * LANES), _F32)

    def scores(kb, slot):
        ks = pl.multiple_of(kb * t, t)
        for pr in range(FOX_PAIRS):
            kt = jnp.concatenate([k_ref[pl.ds(ks, t), lanes_of(pr)],
                                  ka_ref[pl.ds(ks, t), lanes_of(pr)]], axis=1)
            for hd in range(HEADS_PER_TILE):
                h = HEADS_PER_TILE * pr + hd
                s_ref[slot, h] = lax.dot_general(qa_ref[h], kt, _NT, preferred_element_type=_F32)

    def absorb(kb, slot):
        ks = pl.multiple_of(kb * t, t)
        ones = jnp.ones((t, LANES), _BF16)
        for h, (pr, hd) in enumerate(heads):
            vt = jnp.concatenate([v_ref[pl.ds(ks, t), lanes_of(pr)], ones], axis=1)
            s = s_ref[slot, h]
            m = m_ref[h]
            m_new = jnp.maximum(m, jnp.max(s, axis=-1, keepdims=True))
            p = jnp.exp2(s - jnp.concatenate([m_new] * (t // LANES), axis=1)).astype(_BF16)
            alpha = jnp.exp2(m - m_new)
            acc_ref[h] = (jnp.concatenate([alpha, alpha], axis=1) * acc_ref[h]
                          + jnp.dot(p, vt, preferred_element_type=_F32))
            m_ref[h] = m_new

    def absorb_diagonal(kb, slot):
        ks = pl.multiple_of(kb * t, t)
        ones = jnp.ones((t, LANES), _BF16)
        for h, (pr, hd) in enumerate(heads):
            vt = jnp.concatenate([v_ref[pl.ds(ks, t), lanes_of(pr)], ones], axis=1)
            for r0, r1 in ((0, t // 2), (t // 2, t)):
                causal = (lax.broadcasted_iota(jnp.int32, (r1 - r0, r1), 1)
                          <= r0 + lax.broadcasted_iota(jnp.int32, (r1 - r0, r1), 0))
                s = jnp.where(causal, s_ref[slot, h, r0:r1, :r1], MASK_VALUE)
                m = m_ref[h, r0:r1, :]
                m_new = jnp.maximum(m, jnp.max(s, axis=-1, keepdims=True))
                p = jnp.exp2(s - jnp.concatenate([m_new] * (r1 // LANES), axis=1)).astype(_BF16)
                alpha = jnp.exp2(m - m_new)
                acc_ref[h, r0:r1, :] = (jnp.concatenate([alpha, alpha], axis=1) * acc_ref[h, r0:r1, :]
                                        + jnp.dot(p, vt[:r1], preferred_element_type=_F32))
                m_ref[h, r0:r1, :] = m_new

    scores(0, 0)

    def two_blocks(kb):
        scores(kb + 1, 1)
        absorb(kb, 0)
        scores(kb + 2, 0)
        absorb(kb + 1, 1)

    def four_blocks(j, carry):
        two_blocks(4 * j)
        two_blocks(4 * j + 2)
        return carry

    lax.fori_loop(0, lax.shift_right_logical(i, 2), four_blocks, 0)

    @pl.when((i & 2) != 0)
    def _():
        two_blocks(i & ~3)

    @pl.when((i & 1) == 1)
    def _():
        scores(i, 1)
        absorb(i - 1, 0)
        absorb_diagonal(i, 1)

    @pl.when((i & 1) == 0)
    def _():
        absorb_diagonal(i, 0)

    for pr in range(FOX_PAIRS):
        outs = [acc_ref[HEADS_PER_TILE * pr + hd][:, :LANES] / acc_ref[HEADS_PER_TILE * pr + hd][:, LANES:]
                for hd in range(HEADS_PER_TILE)]
        o_ref[:, lanes_of(pr)] = jnp.where(low, outs[0], outs[1]).astype(_BF16)


def _fox_attention(qkv, ka, batch, seq_len):
    t = qkv.shape[0]
    hd = N_HEADS * HEAD_DIM
    n_steps = N_HEADS // (HEADS_PER_TILE * FOX_PAIRS)
    width = FOX_PAIRS * LANES
    heads = FOX_PAIRS * HEADS_PER_TILE
    nq = seq_len // FOX_T
    return pl.pallas_call(
        _fox_kernel,
        grid=(batch, n_steps, nq),
        in_specs=[
            pl.BlockSpec((FOX_T, width), lambda b, p, i: (b * nq + i, p)),
            pl.BlockSpec((seq_len, width), lambda b, p, i: (b, n_steps + p)),
            pl.BlockSpec((seq_len, width), lambda b, p, i: (b, p)),
            pl.BlockSpec((seq_len, width), lambda b, p, i: (b, 2 * n_steps + p)),
        ],
        out_specs=pl.BlockSpec((FOX_T, width), lambda b, p, i: (b * nq + i, p)),
        out_shape=jax.ShapeDtypeStruct((t, hd), _BF16),
        scratch_shapes=[
            pltpu.VMEM((heads, FOX_T, 2 * LANES), _BF16),
            pltpu.VMEM((2, heads, FOX_T, FOX_T), _F32),
            pltpu.VMEM((heads, FOX_T, LANES), _F32),
            pltpu.VMEM((heads, FOX_T, 2 * LANES), _F32),
        ],
        compiler_params=_params(("arbitrary", "arbitrary", "arbitrary")),
        name="fox_attn",
    )(qkv, qkv, ka, qkv)


def _out_kernel(a_ref, w_ref, x_ref, g1_ref, g2_ref, xo_ref, h_ref):
    half = a_ref.shape[0] // 2
    for r in (slice(0, half), slice(half, 2 * half)):
        y = jnp.dot(a_ref[r, :], w_ref[...], preferred_element_type=_F32)
        xn = x_ref[r, :] + y * _rms_scale(y) * g1_ref[...]
        xo_ref[r, :] = xn
        h_ref[r, :] = (xn * _rms_scale(xn) * g2_ref[...]).astype(_BF16)


def _out_proj(attn, w_all, idx, x2, g_post, g_pre_ffn):
    t, d = x2.shape
    k = attn.shape[1]
    row = lambda i: (i, 0)
    fixed = lambda i: (0, 0)
    return pl.pallas_call(
        _out_kernel,
        grid=(t // OUT_TM,),
        in_specs=[
            pl.BlockSpec((OUT_TM, k), row),
            pl.BlockSpec((None, k, d), lambda i: (idx, 0, 0)),
            pl.BlockSpec((OUT_TM, d), row),
            pl.BlockSpec((1, d), fixed),
            pl.BlockSpec((1, d), fixed),
        ],
        out_specs=[pl.BlockSpec((OUT_TM, d), row), pl.BlockSpec((OUT_TM, d), row)],
        out_shape=[jax.ShapeDtypeStruct((t, d), _F32), jax.ShapeDtypeStruct((t, d), _BF16)],
        compiler_params=_params(("arbitrary",)),
        name="out_proj",
    )(attn, w_all, x2, g_post, g_pre_ffn)


def _ffn_kernel(h_ref, wg_ref, wu_ref, wd_ref, x_ref, g_ref, o_ref, acc_ref, *, nf, n_rows):
    s = pl.program_id(0)
    i = s // nf
    f = s - i * nf

    def chunk():
        h = h_ref[...]
        gate = jnp.dot(h, wg_ref[...], preferred_element_type=_F32)
        up = jnp.dot(h, wu_ref[...], preferred_element_type=_F32)
        act = (gate * (1.0 / (1.0 + jnp.exp(-gate))) * up).astype(_BF16)
        return jnp.dot(act, wd_ref[...], preferred_element_type=_F32)

    def finish_previous():
        y = acc_ref[...]
        o_ref[...] = x_ref[...] + y * _rms_scale(y) * g_ref[...]

    @pl.when(f > 0)
    def _():
        acc_ref[...] += chunk()

    @pl.when((f == 0) & (i == 0))
    def _():
        acc_ref[...] = chunk()

    @pl.when((f == 0) & (i > 0) & (i < n_rows))
    def _():
        finish_previous()
        acc_ref[...] = chunk()

    @pl.when(i == n_rows)
    def _():
        finish_previous()


def _ffn(h, w_gu_all, w_down_all, idx, x2, g_post):
    t, d = x2.shape
    d_ff = w_down_all.shape[1]
    nf = d_ff // FFN_TF
    n_rows = t // FFN_TM
    last = n_rows * nf

    def row(s):
        return jnp.minimum(s // nf, n_rows - 1)

    def col(s):
        return jnp.where(s < last, s % nf, nf - 1)

    def prev_row(s):
        return jnp.maximum(s // nf - 1, 0)

    kern = functools.partial(_ffn_kernel, nf=nf, n_rows=n_rows)
    return pl.pallas_call(
        kern,
        grid=(last + 1,),
        in_specs=[
            pl.BlockSpec((FFN_TM, d), lambda s: (row(s), 0)),
            pl.BlockSpec((None, d, FFN_TF), lambda s: (idx, 0, col(s))),
            pl.BlockSpec((None, d, FFN_TF), lambda s: (idx, 0, nf + col(s))),
            pl.BlockSpec((None, FFN_TF, d), lambda s: (idx, col(s), 0)),
            pl.BlockSpec((FFN_TM, d), lambda s: (prev_row(s), 0)),
            pl.BlockSpec((1, d), lambda s: (0, 0)),
        ],
        out_specs=pl.BlockSpec((FFN_TM, d), lambda s: (prev_row(s), 0)),
        out_shape=jax.ShapeDtypeStruct((t, d), _F32),
        scratch_shapes=[pltpu.VMEM((FFN_TM, d), _F32)],
        compiler_params=_params(("arbitrary",)),
        name="ffn",
    )(h, w_gu_all, w_gu_all, w_down_all, x2, g_post)


def _rotary_lane_tables(positions):
    half = ROT_DIM // 2
    inv_freq = ROPE_THETA ** (-jnp.arange(0, ROT_DIM, 2, dtype=_F32) / ROT_DIM)
    ang = positions.astype(_F32).reshape(-1, 1) * inv_freq
    cos, sin = jnp.cos(ang), jnp.sin(ang)
    t = ang.shape[0]
    pad = HEAD_DIM - ROT_DIM
    ra = jnp.concatenate([cos, cos, jnp.ones((t, pad), _F32)], axis=1)
    rb = jnp.concatenate([-sin, jnp.zeros((t, half + pad), _F32)], axis=1)
    rc = jnp.concatenate([jnp.zeros((t, half), _F32), sin, jnp.zeros((t, pad), _F32)], axis=1)
    tile = lambda a: jnp.tile(a, (1, HEADS_PER_TILE))
    return tile(ra), tile(rb), tile(rc)


def kernel(x, positions, norm_gains, swa_w_in, swa_sinks, swa_w_out, fox_w_in, fox_b_f,
           fox_w_out, ffn_w_gate_up, ffn_w_down):
    batch, seq_len, d = x.shape
    depth = norm_gains.shape[0]
    hd = N_HEADS * HEAD_DIM
    assert d == D_MODEL and seq_len % PROJ_TM == 0 and seq_len % FOX_T == 0
    assert WINDOW <= BLOCK

    x2 = x.reshape(batch * seq_len, d)
    ra, rb, rc = _rotary_lane_tables(positions)
    gains = norm_gains.reshape(depth, 4, 1, d)
    swa_w_in_b, swa_w_out_b = swa_w_in.astype(_BF16), swa_w_out.astype(_BF16)
    fox_w_in_t = jnp.swapaxes(fox_w_in, 1, 2)
    fox_w_in_b, fox_w_out_b = fox_w_in_t.astype(_BF16), fox_w_out.astype(_BF16)
    ffn_w_gu_b, ffn_w_down_b = ffn_w_gate_up.astype(_BF16), ffn_w_down.astype(_BF16)

    for layer in range(depth):
        j = layer // 2
        g = gains[layer]
        if layer % 2 == 0:
            qkv = _proj_swa(x2, g[0], swa_w_in_b, j, ra, rb, rc)
            attn = _swa_attention(qkv, swa_sinks[j].astype(_F32), batch, seq_len)
            w_out = swa_w_out_b
        else:
            wf_pad = jnp.pad(fox_w_in_t[j, 3 * hd:, :].astype(_F32), ((0, LANES - N_HEADS), (0, 0)))
            wf_hi = wf_pad.astype(_BF16)
            wf_lo = (wf_pad - wf_hi.astype(_F32)).astype(_BF16)
            bf_pad = jnp.pad(fox_b_f[j].astype(_F32), (0, LANES - N_HEADS)).reshape(1, LANES)
            qkv, ka = _proj_fox(x2, g[0], fox_w_in_b, j,
                                jnp.concatenate([wf_hi, wf_lo], axis=0), bf_pad, seq_len)
            attn = _fox_attention(qkv, ka, batch, seq_len)
            w_out = fox_w_out_b
        x2, h = _out_proj(attn, w_out, j, x2, g[1], g[2])
        x2 = _ffn(h, ffn_w_gu_b, ffn_w_down_b, layer, x2, g[3])
    return x2.reshape(batch, seq_len, d)
```
